```python
import math
import jax, jax.numpy as jnp
from jax import lax
import numpy as np

D_MODEL = 1024
BATCH = 32
SEQ = 2048
DEPTH = 2

HEAD_DIM = 64
ATT_HEADS = 6
ATT_KV_HEADS = 2
ATT_GROUP = ATT_HEADS // ATT_KV_HEADS
ATT_DIM = ATT_HEADS * HEAD_DIM
ATT_KV_DIM = ATT_KV_HEADS * HEAD_DIM
WINDOW = 128
N_BUCKETS = 32
MAX_DISTANCE = 128
RWKV_HEADS = 6
RWKV_DIM = RWKV_HEADS * HEAD_DIM
DECAY_LORA = 32
ICLR_LORA = 32
GATE_LORA = 64
RWKV_COLS = 3 * RWKV_DIM + DECAY_LORA + ICLR_LORA + GATE_LORA
GN_EPS = 64e-5
SSM_GROUPS = 16
SSM_GROUP_CH = 16
SSM_DIM = SSM_GROUPS * SSM_GROUP_CH
SSM_STATE = 64
MIX_WIDTH = ATT_DIM + RWKV_DIM + SSM_DIM
IN_COLS = ATT_DIM + 2 * ATT_KV_DIM + RWKV_COLS + SSM_DIM
D_FF = 2816
RMS_EPS = 1e-6

kernel_name = "hybrid_parallel_heads_block"


def rmsnorm(x, g):
    xf = x.astype(jnp.float32)
    y = xf * lax.rsqrt(jnp.mean(xf * xf, axis=-1, keepdims=True) + RMS_EPS) * g.astype(jnp.float32)
    return y.astype(x.dtype)


def swiglu(x, w_gate, w_up, w_down):
    return (jax.nn.silu(x @ w_gate) * (x @ w_up)) @ w_down


def band_relative_buckets():
    qi = jnp.arange(WINDOW)[:, None]
    kj = jnp.arange(2 * WINDOW)[None, :]
    rel = qi + WINDOW - kj
    in_window = (rel >= 0) & (rel < WINDOW)
    n = jnp.maximum(rel, 0)
    max_exact = N_BUCKETS // 2
    nf = jnp.maximum(n, 1).astype(jnp.float32)
    large = max_exact + (jnp.log(nf / max_exact) / math.log(MAX_DISTANCE / max_exact)
                         * (N_BUCKETS - max_exact)).astype(jnp.int32)
    large = jnp.minimum(large, N_BUCKETS - 1)
    bucket = jnp.where(n < max_exact, n, large)
    return bucket, in_window


def sliding_window_sink_attention(q, k, v, sinks, rel_bias):
    Bsz, T, _ = q.shape
    nb = T // WINDOW
    qb = q.reshape(Bsz, nb, WINDOW, ATT_KV_HEADS, ATT_GROUP, HEAD_DIM)

    def band(z):
        zb = jnp.pad(z, ((0, 0), (WINDOW, 0), (0, 0))).reshape(Bsz, nb + 1, WINDOW, ATT_KV_HEADS, HEAD_DIM)
        return jnp.concatenate([zb[:, :-1], zb[:, 1:]], axis=2)

    kb, vb = band(k), band(v)
    s = jnp.einsum('bnqhgd,bnkhd->bhgnqk', qb, kb,
                   preferred_element_type=jnp.float32) * (HEAD_DIM ** -0.5)
    bucket, in_window = band_relative_buckets()
    bias = jnp.transpose(rel_bias.astype(jnp.float32)[bucket], (2, 0, 1))
    bias = bias.reshape(ATT_KV_HEADS, ATT_GROUP, 1, WINDOW, 2 * WINDOW)
    key_pos = (jnp.arange(nb)[:, None, None] - 1) * WINDOW + jnp.arange(2 * WINDOW)[None, None, :]
    mask = in_window[None] & (key_pos >= 0)
    s = jnp.where(mask, s + bias, -jnp.inf)
    sink = sinks.astype(jnp.float32).reshape(ATT_KV_HEADS, ATT_GROUP, 1, 1, 1)
    m = jnp.maximum(s.max(axis=-1, keepdims=True), sink)
    p = jnp.exp(s - m)
    p = p / (p.sum(axis=-1, keepdims=True) + jnp.exp(sink - m))
    o = jnp.einsum('bhgnqk,bnkhd->bnqhgd', p, vb.astype(jnp.float32))
    return o.reshape(Bsz, T, ATT_DIM)


def rwkv7_time_mix(p, mu, w0, w2, a0, a2, g2, k_k, k_a, r_k, gn_w, gn_b):
    Bsz, T, _ = p.shape
    p = p.astype(jnp.float32)
    prev = jnp.pad(p[:, :-1], ((0, 0), (1, 0), (0, 0)))
    p = p + (prev - p) * mu
    o1 = RWKV_DIM
    o2 = 2 * RWKV_DIM
    o3 = 3 * RWKV_DIM
    o4 = o3 + DECAY_LORA
    o5 = o4 + ICLR_LORA
    r, k, v, wd, ad, gd = jnp.split(p, [o1, o2, o3, o4, o5], axis=-1)
    w = -jax.nn.softplus(-(w0 + jnp.tanh(wd) @ w2)) - 0.5
    decay = jnp.exp(-jnp.exp(w))
    a = jax.nn.sigmoid(a0 + ad @ a2)
    g = jax.nn.sigmoid(gd) @ g2

    def heads(z):
        return z.reshape(Bsz, T, RWKV_HEADS, HEAD_DIM)

    kk = heads(k * k_k)
    kk = kk / jnp.maximum(jnp.sqrt(jnp.sum(kk * kk, axis=-1, keepdims=True)), 1e-12)
    k = k * (1.0 + (a - 1.0) * k_a)
    r_h, k_h, v_h, w_h, a_h = heads(r), heads(k), heads(v), heads(decay), heads(a)
    xs = tuple(jnp.moveaxis(z, 1, 0) for z in (r_h, w_h, k_h, v_h, -kk, kk * a_h))

    def step(S, inp):
        rt, wt, kt, vt, at, bt = inp
        sa = jnp.einsum('bhvk,bhk->bhv', S, at)
        S = S * wt[:, :, None, :] + sa[..., None] * bt[:, :, None, :] + vt[..., None] * kt[:, :, None, :]
        return S, jnp.einsum('bhvk,bhk->bhv', S, rt)

    S0 = jnp.zeros((Bsz, RWKV_HEADS, HEAD_DIM, HEAD_DIM), jnp.float32)
    _, y = lax.scan(step, S0, xs)
    y = jnp.moveaxis(y, 0, 1)
    mean = jnp.mean(y, axis=-1, keepdims=True)
    var = jnp.mean(jnp.square(y - mean), axis=-1, keepdims=True)
    y = ((y - mean) * lax.rsqrt(var + GN_EPS)).reshape(Bsz, T, RWKV_DIM) * gn_w + gn_b
    y = y + (jnp.sum(r_h * k_h * r_k, axis=-1, keepdims=True) * v_h).reshape(Bsz, T, RWKV_DIM)
    return y * g


def s5_ssm(u, lam_re, lam_im, log_dt, b_re, b_im, c_re, c_im, d, glu_w, glu_b):
    Bsz, T, _ = u.shape
    f32 = jnp.float32
    uf = u.astype(f32).reshape(Bsz, T, SSM_GROUPS, SSM_GROUP_CH)
    lr = jnp.minimum(lam_re.astype(f32), -1e-4)
    li = lam_im.astype(f32)
    dt = jnp.exp(log_dt.astype(f32))[:, None]
    mag = jnp.exp(lr * dt)
    abar_re = mag * jnp.cos(li * dt)
    abar_im = mag * jnp.sin(li * dt)
    den = lr * lr + li * li
    nr = abar_re - 1.0
    ni = abar_im
    fr = (nr * lr + ni * li) / den
    fi = (ni * lr - nr * li) / den
    b_re = b_re.astype(f32)
    b_im = b_im.astype(f32)
    bbar_re = fr[..., None] * b_re - fi[..., None] * b_im
    bbar_im = fr[..., None] * b_im + fi[..., None] * b_re
    bu_re = jnp.einsum('btgh,gph->btgp', uf, bbar_re)
    bu_im = jnp.einsum('btgh,gph->btgp', uf, bbar_im)
    a_re = jnp.broadcast_to(abar_re, (1, T, SSM_GROUPS, SSM_STATE))
    a_im = jnp.broadcast_to(abar_im, (1, T, SSM_GROUPS, SSM_STATE))

    def combine(e1, e2):
        a1r, a1i, b1r, b1i = e1
        a2r, a2i, b2r, b2i = e2
        return (a2r * a1r - a2i * a1i, a2r * a1i + a2i * a1r,
                a2r * b1r - a2i * b1i + b2r, a2r * b1i + a2i * b1r + b2i)

    _, _, xr, xi = lax.associative_scan(combine, (a_re, a_im, bu_re, bu_im), axis=1)
    y = (jnp.einsum('btgp,ghp->btgh', xr, c_re.astype(f32))
         - jnp.einsum('btgp,ghp->btgh', xi, c_im.astype(f32))
         + d.astype(f32) * uf)
    y = jax.nn.gelu(y.reshape(Bsz, T, SSM_DIM))
    return y * jax.nn.sigmoid(y @ glu_w.astype(f32) + glu_b.astype(f32))


def hybrid_mixer(xn, w_in, w_out, att_sinks, rel_bias,
                 rwkv_mu, rwkv_w0, rwkv_w2, rwkv_a0, rwkv_a2, rwkv_g2,
                 rwkv_k_k, rwkv_k_a, rwkv_r_k, rwkv_gn_w, rwkv_gn_b,
                 ssm_lambda_re, ssm_lambda_im, ssm_log_dt, ssm_b_re, ssm_b_im,
                 ssm_c_re, ssm_c_im, ssm_d, ssm_glu_w, ssm_glu_b):
    p = xn @ w_in
    c1 = ATT_DIM
    c2 = c1 + ATT_KV_DIM
    c3 = c2 + ATT_KV_DIM
    c4 = c3 + RWKV_COLS
    q, k, v, rw, su = jnp.split(p, [c1, c2, c3, c4], axis=-1)
    y_att = sliding_window_sink_attention(q, k, v, att_sinks, rel_bias)
    y_rwkv = rwkv7_time_mix(rw, rwkv_mu, rwkv_w0, rwkv_w2, rwkv_a0, rwkv_a2, rwkv_g2,
                            rwkv_k_k, rwkv_k_a, rwkv_r_k, rwkv_gn_w, rwkv_gn_b)
    y_ssm = s5_ssm(su, ssm_lambda_re, ssm_lambda_im, ssm_log_dt, ssm_b_re, ssm_b_im,
                   ssm_c_re, ssm_c_im, ssm_d, ssm_glu_w, ssm_glu_b)
    dt = xn.dtype
    y = jnp.concatenate([y_att.astype(dt), y_rwkv.astype(dt), y_ssm.astype(dt)], axis=-1)
    return y @ w_out


def setup_inputs(seed: int = 0) -> dict:
    key = jax.random.key(seed)
    ks = iter(jax.random.split(key, 64))
    f32 = jnp.float32
    L = DEPTH

    def nrm(shape, scale):
        return scale * jax.random.normal(next(ks), shape, f32)

    def gain(shape):
        return 1.0 + 0.02 * jax.random.normal(next(ks), shape, f32)

    x = jax.random.normal(next(ks), (BATCH, SEQ, D_MODEL), f32)
    rel_bias = nrm((N_BUCKETS, ATT_HEADS), 0.2)
    ln_pre_ffn1 = gain((L, D_MODEL))
    ln_post_ffn1 = gain((L, D_MODEL))
    ffn1_w_gate = nrm((L, D_MODEL, D_FF), D_MODEL ** -0.5)
    ffn1_w_up = nrm((L, D_MODEL, D_FF), D_MODEL ** -0.5)
    ffn1_w_down = nrm((L, D_FF, D_MODEL), D_FF ** -0.5)
    ln_pre_mix = gain((L, D_MODEL))
    ln_post_mix = gain((L, D_MODEL))
    w_in = nrm((L, D_MODEL, IN_COLS), D_MODEL ** -0.5)
    w_out = nrm((L, MIX_WIDTH, D_MODEL), MIX_WIDTH ** -0.5)
    att_sinks = nrm((L, ATT_HEADS), 0.5)
    rwkv_mu = jax.random.uniform(next(ks), (L, RWKV_COLS), f32)
    ratio = jnp.arange(RWKV_DIM, dtype=f32) / (RWKV_DIM - 1)
    decay_speed = -7.0 + 5.0 * ratio ** (0.85 + 0.5 ** 0.5)
    rwkv_w0 = jnp.broadcast_to(decay_speed + 0.5, (L, RWKV_DIM)) + nrm((L, RWKV_DIM), 0.01)
    rwkv_w2 = nrm((L, DECAY_LORA, RWKV_DIM), 0.1 * DECAY_LORA ** -0.5)
    rwkv_a0 = nrm((L, RWKV_DIM), 0.1)
    rwkv_a2 = nrm((L, ICLR_LORA, RWKV_DIM), 0.5 * ICLR_LORA ** -0.5)
    rwkv_g2 = nrm((L, GATE_LORA, RWKV_DIM), GATE_LORA ** -0.5)
    rwkv_k_k = 0.85 + nrm((L, RWKV_DIM), 0.02)
    rwkv_k_a = 1.0 + nrm((L, RWKV_DIM), 0.02)
    rwkv_r_k = -0.04 + nrm((L, RWKV_HEADS, HEAD_DIM), 0.02)
    rwkv_gn_w = gain((L, RWKV_DIM))
    rwkv_gn_b = nrm((L, RWKV_DIM), 0.01)
    ssm_lambda_re = -0.5 + nrm((L, SSM_GROUPS, SSM_STATE), 0.01)
    ssm_lambda_im = (jnp.broadcast_to(math.pi * jnp.arange(SSM_STATE, dtype=f32), (L, SSM_GROUPS, SSM_STATE))
                     + nrm((L, SSM_GROUPS, SSM_STATE), 0.01))
    ssm_log_dt = jax.random.uniform(next(ks), (L, SSM_GROUPS), f32,
                                    minval=math.log(0.001), maxval=math.log(0.1))
    ssm_b_re = nrm((L, SSM_GROUPS, SSM_STATE, SSM_GROUP_CH), (2 * SSM_GROUP_CH) ** -0.5)
    ssm_b_im = nrm((L, SSM_GROUPS, SSM_STATE, SSM_GROUP_CH), (2 * SSM_GROUP_CH) ** -0.5)
    ssm_c_re = nrm((L, SSM_GROUPS, SSM_GROUP_CH, SSM_STATE), SSM_STATE ** -0.5)
    ssm_c_im = nrm((L, SSM_GROUPS, SSM_GROUP_CH, SSM_STATE), SSM_STATE ** -0.5)
    ssm_d = nrm((L, SSM_GROUPS, SSM_GROUP_CH), 1.0)
    ssm_glu_w = nrm((L, SSM_DIM, SSM_DIM), SSM_DIM ** -0.5)
    ssm_glu_b = nrm((L, SSM_DIM), 0.01)
    ln_pre_ffn2 = gain((L, D_MODEL))
    ln_post_ffn2 = gain((L, D_MODEL))
    ffn2_w_gate = nrm((L, D_MODEL, D_FF), D_MODEL ** -0.5)
    ffn2_w_up = nrm((L, D_MODEL, D_FF), D_MODEL ** -0.5)
    ffn2_w_down = nrm((L, D_FF, D_MODEL), D_FF ** -0.5)
    return {
        "x": x, "rel_bias": rel_bias,
        "ln_pre_ffn1": ln_pre_ffn1, "ln_post_ffn1": ln_post_ffn1,
        "ffn1_w_gate": ffn1_w_gate, "ffn1_w_up": ffn1_w_up, "ffn1_w_down": ffn1_w_down,
        "ln_pre_mix": ln_pre_mix, "ln_post_mix": ln_post_mix,
        "w_in": w_in, "w_out": w_out, "att_sinks": att_sinks,
        "rwkv_mu": rwkv_mu, "rwkv_w0": rwkv_w0, "rwkv_w2": rwkv_w2,
        "rwkv_a0": rwkv_a0, "rwkv_a2": rwkv_a2, "rwkv_g2": rwkv_g2,
        "rwkv_k_k": rwkv_k_k, "rwkv_k_a": rwkv_k_a, "rwkv_r_k": rwkv_r_k,
        "rwkv_gn_w": rwkv_gn_w, "rwkv_gn_b": rwkv_gn_b,
        "ssm_lambda_re": ssm_lambda_re, "ssm_lambda_im": ssm_lambda_im, "ssm_log_dt": ssm_log_dt,
        "ssm_b_re": ssm_b_re, "ssm_b_im": ssm_b_im, "ssm_c_re": ssm_c_re, "ssm_c_im": ssm_c_im,
        "ssm_d": ssm_d, "ssm_glu_w": ssm_glu_w, "ssm_glu_b": ssm_glu_b,
        "ln_pre_ffn2": ln_pre_ffn2, "ln_post_ffn2": ln_post_ffn2,
        "ffn2_w_gate": ffn2_w_gate, "ffn2_w_up": ffn2_w_up, "ffn2_w_down": ffn2_w_down,
    }


def reference(x, rel_bias,
              ln_pre_ffn1, ln_post_ffn1, ffn1_w_gate, ffn1_w_up, ffn1_w_down,
              ln_pre_mix, ln_post_mix, w_in, w_out, att_sinks,
              rwkv_mu, rwkv_w0, rwkv_w2, rwkv_a0, rwkv_a2, rwkv_g2,
              rwkv_k_k, rwkv_k_a, rwkv_r_k, rwkv_gn_w, rwkv_gn_b,
              ssm_lambda_re, ssm_lambda_im, ssm_log_dt, ssm_b_re, ssm_b_im, ssm_c_re, ssm_c_im,
              ssm_d, ssm_glu_w, ssm_glu_b,
              ln_pre_ffn2, ln_post_ffn2, ffn2_w_gate, ffn2_w_up, ffn2_w_down):
    h = x
    for l in range(DEPTH):
        f = swiglu(rmsnorm(h, ln_pre_ffn1[l]), ffn1_w_gate[l], ffn1_w_up[l], ffn1_w_down[l])
        h = h + 0.5 * rmsnorm(f, ln_post_ffn1[l])
        m = hybrid_mixer(rmsnorm(h, ln_pre_mix[l]), w_in[l], w_out[l], att_sinks[l], rel_bias,
                         rwkv_mu[l], rwkv_w0[l], rwkv_w2[l], rwkv_a0[l], rwkv_a2[l], rwkv_g2[l],
                         rwkv_k_k[l], rwkv_k_a[l], rwkv_r_k[l], rwkv_gn_w[l], rwkv_gn_b[l],
                         ssm_lambda_re[l], ssm_lambda_im[l], ssm_log_dt[l], ssm_b_re[l], ssm_b_im[l],
                         ssm_c_re[l], ssm_c_im[l], ssm_d[l], ssm_glu_w[l], ssm_glu_b[l])
        h = h + rmsnorm(m, ln_post_mix[l])
        f = swiglu(rmsnorm(h, ln_pre_ffn2[l]), ffn2_w_gate[l], ffn2_w_up[l], ffn2_w_down[l])
        h = h + 0.5 * rmsnorm(f, ln_post_ffn2[l])
    return h
```

```python
import functools
import math

import jax
import jax.numpy as jnp
from jax import lax
from jax.experimental import pallas as pl
from jax.experimental.pallas import tpu as pltpu

F32 = jnp.float32
BF16 = jnp.bfloat16

HEAD_DIM = 64
ATT_HEADS = 6
ATT_KV_HEADS = 2
ATT_GROUP = ATT_HEADS // ATT_KV_HEADS
ATT_DIM = ATT_HEADS * HEAD_DIM
ATT_KV_DIM = ATT_KV_HEADS * HEAD_DIM
WINDOW = 128
N_BUCKETS = 32
MAX_DISTANCE = 128
RWKV_HEADS = 6
RWKV_DIM = RWKV_HEADS * HEAD_DIM
DECAY_LORA = 32
ICLR_LORA = 32
GATE_LORA = 64
LORA_COLS = DECAY_LORA + ICLR_LORA + GATE_LORA
RWKV_COLS = 3 * RWKV_DIM + LORA_COLS
GN_EPS = 64e-5
SSM_GROUPS = 16
SSM_GROUP_CH = 16
SSM_DIM = SSM_GROUPS * SSM_GROUP_CH
SSM_STATE = 64
RMS_EPS = 1e-6
MASK_VALUE = -1e30

VMEM_LIMIT_BYTES = 56 * 1024 * 1024
RWKV_CHUNK = 64
SSM_CHUNK = 16


def _params(*sem):
    return pltpu.CompilerParams(dimension_semantics=sem, vmem_limit_bytes=VMEM_LIMIT_BYTES)


def _rms(x, g):
    return x * lax.rsqrt(jnp.mean(x * x, axis=-1, keepdims=True) + RMS_EPS) * g


def _const_spec(shape):
    nd = len(shape)
    return pl.BlockSpec(shape, lambda *_: (0,) * nd)


def _ffn_kernel(x_ref, gpre_ref, gpost_ref, wg_ref, wu_ref, wd_ref, o_ref, acc_ref, *, ff_chunk, n_chunks):
    x = x_ref[...]
    xn = _rms(x, gpre_ref[...]).astype(BF16)
    for c in range(n_chunks):
        sl = slice(c * ff_chunk, (c + 1) * ff_chunk)
        g = jnp.dot(xn, wg_ref[:, sl], preferred_element_type=F32)
        u = jnp.dot(xn, wu_ref[:, sl], preferred_element_type=F32)
        a = (g * jax.nn.sigmoid(g) * u).astype(BF16)
        d = jnp.dot(a, wd_ref[sl, :], preferred_element_type=F32)
        if c == 0:
            acc_ref[...] = d
        else:
            acc_ref[...] += d
    o_ref[...] = x + 0.5 * _rms(acc_ref[...], gpost_ref[...])


def _ffn_call(h, g_pre, g_post, wg, wu, wd, *, tm, ff_chunk):
    n, d = h.shape
    dff = wg.shape[1]
    kern = functools.partial(_ffn_kernel, ff_chunk=ff_chunk, n_chunks=dff // ff_chunk)
    return pl.pallas_call(
        kern,
        grid=(n // tm,),
        in_specs=[
            pl.BlockSpec((tm, d), lambda i: (i, 0)),
            _const_spec((1, d)), _const_spec((1, d)),
            _const_spec((d, dff)), _const_spec((d, dff)), _const_spec((dff, d)),
        ],
        out_specs=pl.BlockSpec((tm, d), lambda i: (i, 0)),
        out_shape=jax.ShapeDtypeStruct((n, d), F32),
        scratch_shapes=[pltpu.VMEM((tm, d), F32)],
        compiler_params=_params("parallel"),
    )(h, g_pre.reshape(1, d), g_post.reshape(1, d), wg, wu, wd)


_IN_SPLITS = (ATT_DIM, ATT_KV_DIM, ATT_KV_DIM, RWKV_COLS, SSM_DIM)


def _inproj_kernel(x_ref, g_ref, w_ref, q_ref, k_ref, v_ref, rw_ref, su_ref):
    xn = _rms(x_ref[...], g_ref[...]).astype(BF16)
    off = 0
    for ref, width in zip((q_ref, k_ref, v_ref, rw_ref, su_ref), _IN_SPLITS):
        ref[...] = jnp.dot(xn, w_ref[:, off:off + width], preferred_element_type=F32)
        off += width


def _inproj_call(h, g_pre, w_in, *, tm):
    n, d = h.shape
    cols = w_in.shape[1]
    return pl.pallas_call(
        _inproj_kernel,
        grid=(n // tm,),
        in_specs=[pl.BlockSpec((tm, d), lambda i: (i, 0)), _const_spec((1, d)), _const_spec((d, cols))],
        out_specs=[pl.BlockSpec((tm, w), lambda i: (i, 0)) for w in _IN_SPLITS],
        out_shape=[jax.ShapeDtypeStruct((n, w), F32) for w in _IN_SPLITS],
        compiler_params=_params("parallel"),
    )(h, g_pre.reshape(1, d), w_in)


def _band_bias(rel_bias):
    qi = jnp.arange(WINDOW)[:, None]
    kj = jnp.arange(2 * WINDOW)[None, :]
    rel = qi + WINDOW - kj
    in_window = (rel >= 0) & (rel < WINDOW)
    n = jnp.maximum(rel, 0)
    max_exact = N_BUCKETS // 2
    nf = jnp.maximum(n, 1).astype(F32)
    large = max_exact + (jnp.log(nf / max_exact) / math.log(MAX_DISTANCE / max_exact)
                         * (N_BUCKETS - max_exact)).astype(jnp.int32)
    large = jnp.minimum(large, N_BUCKETS - 1)
    bucket = jnp.where(n < max_exact, n, large)
    bias = jnp.transpose(rel_bias.astype(F32)[bucket], (2, 0, 1))
    return jnp.where(in_window[None], bias, MASK_VALUE)


def _attn_kernel(sink_ref, q_ref, kp_ref, kc_ref, vp_ref, vc_ref, bias_ref, o_ref):
    blk = pl.program_id(1)
    q = q_ref[...] * (HEAD_DIM ** -0.5)
    k = jnp.concatenate([kp_ref[...], kc_ref[...]], axis=0).astype(BF16)
    v = jnp.concatenate([vp_ref[...], vc_ref[...]], axis=0).astype(BF16)
    col = lax.broadcasted_iota(jnp.int32, (1, 2 * WINDOW), 1)
    key_exists = jnp.logical_or(blk > 0, col >= WINDOW)
    outs = []
    for j in range(ATT_KV_HEADS):
        kj = k[:, j * HEAD_DIM:(j + 1) * HEAD_DIM]
        vj = v[:, j * HEAD_DIM:(j + 1) * HEAD_DIM]
        for gi in range(ATT_GROUP):
            h = j * ATT_GROUP + gi
            qh = q[:, h * HEAD_DIM:(h + 1) * HEAD_DIM].astype(BF16)
            s = lax.dot_general(qh, kj, (((1,), (1,)), ((), ())), preferred_element_type=F32)
            s = jnp.where(key_exists, s + bias_ref[h], MASK_VALUE)
            sink = sink_ref[h]
            m = jnp.maximum(jnp.max(s, axis=-1, keepdims=True), sink)
            p = jnp.exp(s - m)
            denom = jnp.sum(p, axis=-1, keepdims=True) + jnp.exp(sink - m)
            o = jnp.dot(p.astype(BF16), vj, preferred_element_type=F32)
            outs.append(o / denom)
    o_ref[...] = jnp.concatenate(outs, axis=-1)


def _attn_call(q, k, v, bias, sinks):
    bsz, t, _ = q.shape
    nb = t // WINDOW
    cur = lambda b, n: (b, n, 0)
    prv = lambda b, n: (b, jnp.maximum(n - 1, 0), 0)
    kv_blk = (None, WINDOW, ATT_KV_DIM)
    return pl.pallas_call(
        _attn_kernel,
        grid=(bsz, nb),
        in_specs=[
            pl.BlockSpec(memory_space=pltpu.SMEM),
            pl.BlockSpec((None, WINDOW, ATT_DIM), cur),
            pl.BlockSpec(kv_blk, prv), pl.BlockSpec(kv_blk, cur),
            pl.BlockSpec(kv_blk, prv), pl.BlockSpec(kv_blk, cur),
            _const_spec((ATT_HEADS, WINDOW, 2 * WINDOW)),
        ],
        out_specs=pl.BlockSpec((None, WINDOW, ATT_DIM), cur),
        out_shape=jax.ShapeDtypeStruct((bsz, t, ATT_DIM), F32),
        compiler_params=_params("parallel", "parallel"),
    )(sinks, q, k, k, v, v, bias)


def _bdot(a, b, contract):
    return lax.dot_general(a.astype(BF16), b.astype(BF16), (contract, ((0,), (0,))),
                           preferred_element_type=F32)


def _bmm(a, b):
    return _bdot(a, b, ((2,), (1,)))


def _bmm_nt(a, b):
    return _bdot(a, b, ((2,), (2,)))


def _bmm_tn(a, b):
    return _bdot(a, b, ((1,), (1,)))


def _split_heads(x):
    return jnp.stack([x[:, h * HEAD_DIM:(h + 1) * HEAD_DIM] for h in range(RWKV_HEADS)], axis=0)


def _merge_heads(x):
    return jnp.concatenate([x[h] for h in range(RWKV_HEADS)], axis=-1)


def _rwkv_kernel(p_ref, mu_ref, w0_ref, a0_ref, kk_ref, ka_ref, rk_ref, gnw_ref, gnb_ref,
                 lora_ref, hb_ref, tril_ref, o_ref, prev_ref, s_ref):
    C, D = RWKV_CHUNK, RWKV_DIM

    @pl.when(pl.program_id(1) == 0)
    def _():
        prev_ref[...] = jnp.zeros_like(prev_ref)
        s_ref[...] = jnp.zeros_like(s_ref)

    p = p_ref[...]
    row = lax.broadcasted_iota(jnp.int32, p.shape, 0)
    prev = jnp.where(row == 0, prev_ref[...], pltpu.roll(p, 1, 0))
    prev_ref[...] = p[C - 1:C, :]
    xs = p + (prev - p) * mu_ref[...]
    r, k, v = xs[:, 0:D], xs[:, D:2 * D], xs[:, 2 * D:3 * D]
    lo = xs[:, 3 * D:3 * D + LORA_COLS]
    lane = lax.broadcasted_iota(jnp.int32, lo.shape, 1)
    act = jnp.where(lane < DECAY_LORA, jnp.tanh(lo),
                    jnp.where(lane < DECAY_LORA + ICLR_LORA, lo, jax.nn.sigmoid(lo)))
    proj = jnp.dot(act.astype(BF16), lora_ref[...], preferred_element_type=F32)
    z = -(w0_ref[...] + proj[:, 0:D])
    softplus = jnp.maximum(z, 0.0) + jnp.log(1.0 + jnp.exp(-jnp.abs(z)))
    logw = -jnp.exp(-softplus - 0.5)
    a = jax.nn.sigmoid(a0_ref[...] + proj[:, D:2 * D])
    g = proj[:, 2 * D:3 * D]
    hb = hb_ref[...]
    kk = k * kk_ref[...]
    ss = jnp.dot((kk * kk).astype(BF16), hb, preferred_element_type=F32)
    kk = kk / jnp.maximum(jnp.sqrt(ss), 1e-12)
    k = k * (1.0 + (a - 1.0) * ka_ref[...])

    cum = jnp.dot(tril_ref[...], logw, precision=lax.Precision.HIGHEST, preferred_element_type=F32)
    e_pos = jnp.exp(cum)
    e_neg = jnp.exp(-cum)
    g_end = e_pos[C - 1:C, :]
    rt = _split_heads(r * e_pos)
    kt = _split_heads(k * e_neg)
    bt = _split_heads(kk * a * e_neg)
    at = _split_heads(-kk * jnp.exp(cum - logw))
    ktg = _split_heads(k * e_neg * g_end)
    btg = _split_heads(kk * a * e_neg * g_end)
    vh = _split_heads(v)

    ti = lax.broadcasted_iota(jnp.int32, (C, C), 0)
    si = lax.broadcasted_iota(jnp.int32, (C, C), 1)
    strict = (ti > si)[None]
    incl = (ti >= si)[None]
    big = _bmm_nt(jnp.concatenate([at, rt], axis=1), jnp.concatenate([bt, kt], axis=1))
    a_ab = jnp.where(strict, big[:, :C, :C], 0.0)
    a_ak = jnp.where(strict, big[:, :C, C:], 0.0)
    a_rb = jnp.where(incl, big[:, C:, :C], 0.0)
    a_rk = jnp.where(incl, big[:, C:, C:], 0.0)

    eye = (ti == si)[None].astype(F32)
    inv = eye + a_ab
    powr = a_ab
    span = 1
    while span * 2 < C:
        powr = _bmm(powr, powr)
        inv = inv + _bmm(powr, inv)
        span *= 2
    akv = _bmm(a_ak, vh)
    hat = _bmm(inv, jnp.concatenate([at, akv], axis=-1))
    a_hat, v_hat = hat[..., :HEAD_DIM], hat[..., HEAD_DIM:]

    s = s_ref[...]
    sr = _bmm_nt(jnp.concatenate([a_hat, rt], axis=1), s)
    u = sr[:, :C] + v_hat
    uv = jnp.concatenate([u, vh], axis=1)
    y = sr[:, C:] + _bmm(jnp.concatenate([a_rb, a_rk], axis=-1), uv)
    s_ref[...] = s * _split_heads(g_end) + _bmm_tn(uv, jnp.concatenate([btg, ktg], axis=1))

    y = _merge_heads(y)
    inv_n = 1.0 / HEAD_DIM
    mean = jnp.dot(y.astype(BF16), hb, preferred_element_type=F32) * inv_n
    yc = y - mean
    var = jnp.dot((yc * yc).astype(BF16), hb, preferred_element_type=F32) * inv_n
    y = yc * lax.rsqrt(var + GN_EPS) * gnw_ref[...] + gnb_ref[...]
    rk = jnp.dot((r * k * rk_ref[...]).astype(BF16), hb, preferred_element_type=F32)
    o_ref[...] = (y + rk * v) * g


def _rwkv_call(rw, mu, w0, w2, a0, a2, g2, k_k, k_a, r_k, gn_w, gn_b):
    bsz, t, cols = rw.shape
    C, D = RWKV_CHUNK, RWKV_DIM
    lora = jnp.zeros((LORA_COLS, 3 * D), F32)
    lora = lora.at[0:DECAY_LORA, 0:D].set(w2)
    lora = lora.at[DECAY_LORA:DECAY_LORA + ICLR_LORA, D:2 * D].set(a2)
    lora = lora.at[DECAY_LORA + ICLR_LORA:, 2 * D:].set(g2)
    head_id = jnp.arange(D) // HEAD_DIM
    hb = (head_id[:, None] == head_id[None, :]).astype(BF16)
    tril = (jnp.arange(C)[:, None] >= jnp.arange(C)[None, :]).astype(F32)
    vec = lambda x: x.reshape(1, -1).astype(F32)
    vecs = [vec(mu), vec(w0), vec(a0), vec(k_k), vec(k_a), vec(r_k), vec(gn_w), vec(gn_b)]
    consts = vecs + [lora.astype(BF16), hb, tril]
    return pl.pallas_call(
        _rwkv_kernel,
        grid=(bsz, t // C),
        in_specs=[pl.BlockSpec((None, C, cols), lambda b, c: (b, c, 0))] + [_const_spec(x.shape) for x in consts],
        out_specs=pl.BlockSpec((None, C, D), lambda b, c: (b, c, 0)),
        out_shape=jax.ShapeDtypeStruct((bsz, t, D), F32),
        scratch_shapes=[pltpu.VMEM((1, cols), F32), pltpu.VMEM((RWKV_HEADS, HEAD_DIM, HEAD_DIM), F32)],
        compiler_params=_params("parallel", "arbitrary"),
    )(rw, *consts)


def _s5_operators(lam_re, lam_im, log_dt, b_re, b_im, c_re, c_im, d):
    L, G, P, H = SSM_CHUNK, SSM_GROUPS, SSM_STATE, SSM_GROUP_CH
    lr = jnp.minimum(lam_re.astype(F32), -1e-4)
    li = lam_im.astype(F32)
    dt = jnp.exp(log_dt.astype(F32))[:, None]
    n = jnp.arange(L + 1, dtype=F32)[:, None, None]
    mag = jnp.exp(lr * dt * n)
    pr = mag * jnp.cos(li * dt * n)
    pi = mag * jnp.sin(li * dt * n)
    den = lr * lr + li * li
    nr, ni = pr[1] - 1.0, pi[1]
    fr = (nr * lr + ni * li) / den
    fi = (ni * lr - nr * li) / den
    b_re, b_im = b_re.astype(F32), b_im.astype(F32)
    bb_re = fr[..., None] * b_re - fi[..., None] * b_im
    bb_im = fr[..., None] * b_im + fi[..., None] * b_re
    c_re, c_im = c_re.astype(F32), c_im.astype(F32)
    cl_re = c_re[None] * pr[:, :, None, :] - c_im[None] * pi[:, :, None, :]
    cl_im = c_re[None] * pi[:, :, None, :] + c_im[None] * pr[:, :, None, :]
    kern = (jnp.einsum('nghp,gpk->nghk', cl_re[:L], bb_re, precision=lax.Precision.HIGHEST)
            - jnp.einsum('nghp,gpk->nghk', cl_im[:L], bb_im, precision=lax.Precision.HIGHEST))
    s_idx = jnp.arange(L)[:, None]
    t_idx = jnp.arange(L)[None, :]
    lag = jnp.clip(t_idx - s_idx, 0, L - 1)
    toep = jnp.where((t_idx >= s_idx)[:, :, None, None, None], kern[lag], 0.0)
    toep = jnp.transpose(toep, (2, 0, 4, 1, 3)).reshape(G, L * H, L * H)
    rev_r, rev_i = pr[:L][::-1], pi[:L][::-1]
    be_re = rev_r[..., None] * bb_re[None] - rev_i[..., None] * bb_im[None]
    be_im = rev_r[..., None] * bb_im[None] + rev_i[..., None] * bb_re[None]
    to_rows = lambda x: jnp.transpose(x, (1, 0, 3, 2)).reshape(G, L * H, P)
    b_end = jnp.concatenate([to_rows(be_re), to_rows(be_im)], axis=-1)
    to_cols = lambda x: jnp.transpose(x, (1, 3, 0, 2)).reshape(G, P, L * H)
    c_out = jnp.concatenate([to_cols(cl_re[1:]), to_cols(-cl_im[1:])], axis=1)
    a_same = jnp.concatenate([pr[L], pr[L]], axis=-1)[:, None, :]
    a_swap = jnp.concatenate([-pi[L], pi[L]], axis=-1)[:, None, :]
    d_row = jnp.tile(d.astype(F32), (1, L))[:, None, :]
    return toep.astype(BF16), b_end.astype(BF16), c_out.astype(BF16), a_same, a_swap, d_row


def _s5_kernel(u_ref, toep_ref, bend_ref, cout_ref, asame_ref, aswap_ref, d_ref, y_ref, e_ref, xp_ref,
               *, n_chunks, bsz):
    u = u_ref[...]
    ub = u.astype(BF16)
    e_ref[...] = jnp.dot(ub, bend_ref[...], preferred_element_type=F32)
    a_same = jnp.broadcast_to(asame_ref[...], (bsz, 2 * SSM_STATE))
    a_swap = jnp.broadcast_to(aswap_ref[...], (bsz, 2 * SSM_STATE))

    def step(c, x):
        rows = pl.ds(pl.multiple_of(c * bsz, bsz), bsz)
        xp_ref[rows, :] = x
        return a_same * x + a_swap * pltpu.roll(x, SSM_STATE, 1) + e_ref[rows, :]

    lax.fori_loop(0, n_chunks, step, jnp.zeros((bsz, 2 * SSM_STATE), F32))
    y = (jnp.dot(ub, toep_ref[...], preferred_element_type=F32)
         + jnp.dot(xp_ref[...].astype(BF16), cout_ref[...], preferred_element_type=F32)
         + d_ref[...] * u)
    y_ref[...] = jax.nn.gelu(y)


def _s5_call(su, ops):
    bsz, t, _ = su.shape
    L, G, H = SSM_CHUNK, SSM_GROUPS, SSM_GROUP_CH
    nck = t // L
    rows, width = nck * bsz, L * H
    u = su.reshape(bsz, nck, L, G, H).transpose(3, 1, 0, 2, 4).reshape(G, rows, width)
    kern = functools.partial(_s5_kernel, n_chunks=nck, bsz=bsz)
    per_g = lambda *shape: pl.BlockSpec((None,) + shape, lambda g: (g, 0, 0))
    y = pl.pallas_call(
        kern,
        grid=(G,),
        in_specs=[per_g(rows, width), per_g(width, width), per_g(width, 2 * SSM_STATE),
                  per_g(2 * SSM_STATE, width), per_g(1, 2 * SSM_STATE), per_g(1, 2 * SSM_STATE), per_g(1, width)],
        out_specs=per_g(rows, width),
        out_shape=jax.ShapeDtypeStruct((G, rows, width), F32),
        scratch_shapes=[pltpu.VMEM((rows, 2 * SSM_STATE), F32), pltpu.VMEM((rows, 2 * SSM_STATE), F32)],
        compiler_params=_params("parallel"),
    )(u, *ops)
    return y.reshape(G, nck, bsz, L, H).transpose(2, 1, 3, 0, 4).reshape(bsz, t, SSM_DIM)


def _outproj_kernel(h_ref, ya_ref, yr_ref, ys_ref, gluw_ref, glub_ref, wout_ref, g_ref, o_ref):
    ys = ys_ref[...]
    gate = jnp.dot(ys.astype(BF16), gluw_ref[...], preferred_element_type=F32) + glub_ref[...]
    ys = ys * jax.nn.sigmoid(gate)
    y = jnp.concatenate([ya_ref[...], yr_ref[...], ys], axis=-1).astype(BF16)
    m = jnp.dot(y, wout_ref[...], preferred_element_type=F32)
    o_ref[...] = h_ref[...] + _rms(m, g_ref[...])


def _outproj_call(h, ya, yr, ys, glu_w, glu_b, w_out, g_post, *, tm):
    n, d = h.shape
    row = lambda w: pl.BlockSpec((tm, w), lambda i: (i, 0))
    return pl.pallas_call(
        _outproj_kernel,
        grid=(n // tm,),
        in_specs=[row(d), row(ATT_DIM), row(RWKV_DIM), row(SSM_DIM),
                  _const_spec(glu_w.shape), _const_spec((1, SSM_DIM)), _const_spec(w_out.shape), _const_spec((1, d))],
        out_specs=row(d),
        out_shape=jax.ShapeDtypeStruct((n, d), F32),
        compiler_params=_params("parallel"),
    )(h, ya, yr, ys, glu_w, glu_b.reshape(1, -1), w_out, g_post.reshape(1, d))


def kernel(x, rel_bias, ln_pre_ffn1, ln_post_ffn1, ffn1_w_gate, ffn1_w_up, ffn1_w_down, ln_pre_mix, ln_post_mix, w_in, w_out, att_sinks, rwkv_mu, rwkv_w0, rwkv_w2, rwkv_a0, rwkv_a2, rwkv_g2, rwkv_k_k, rwkv_k_a, rwkv_r_k, rwkv_gn_w, rwkv_gn_b, ssm_lambda_re, ssm_lambda_im, ssm_log_dt, ssm_b_re, ssm_b_im, ssm_c_re, ssm_c_im, ssm_d, ssm_glu_w, ssm_glu_b, ln_pre_ffn2, ln_post_ffn2, ffn2_w_gate, ffn2_w_up, ffn2_w_down):
    bsz, t, d = x.shape
    n = bsz * t
    depth = w_in.shape[0]
    tm = 512 if n % 512 == 0 else 128
    ff_chunk = 256
    bf = lambda w: w.astype(BF16)
    bias = _band_bias(rel_bias)
    h = x.reshape(n, d)
    for l in range(depth):
        h = _ffn_call(h, ln_pre_ffn1[l], ln_post_ffn1[l], bf(ffn1_w_gate[l]), bf(ffn1_w_up[l]), bf(ffn1_w_down[l]),
                      tm=tm, ff_chunk=ff_chunk)
        q, k, v, rw, su = _inproj_call(h, ln_pre_mix[l], bf(w_in[l]), tm=tm)
        seq = lambda z: z.reshape(bsz, t, z.shape[-1])
        ya = _attn_call(seq(q), seq(k), seq(v), bias, att_sinks[l].astype(F32))
        yr = _rwkv_call(seq(rw), rwkv_mu[l], rwkv_w0[l], rwkv_w2[l], rwkv_a0[l], rwkv_a2[l], rwkv_g2[l],
                        rwkv_k_k[l], rwkv_k_a[l], rwkv_r_k[l], rwkv_gn_w[l], rwkv_gn_b[l])
        ops = _s5_operators(ssm_lambda_re[l], ssm_lambda_im[l], ssm_log_dt[l], ssm_b_re[l], ssm_b_im[l],
                            ssm_c_re[l], ssm_c_im[l], ssm_d[l])
        ys = _s5_call(seq(su), ops)
        h = _outproj_call(h, ya.reshape(n, -1), yr.reshape(n, -1), ys.reshape(n, -1),
                          bf(ssm_glu_w[l]), ssm_glu_b[l].astype(F32), bf(w_out[l]), ln_post_mix[l], tm=tm)
        h = _ffn_call(h, ln_pre_ffn2[l], ln_post_ffn2[l], bf(ffn2_w_gate[l]), bf(ffn2_w_up[l]), bf(ffn2_w_down[l]),
                      tm=tm, ff_chunk=ff_chunk)
    return h.reshape(bsz, t, d)
```

```python
import functools
import math

import jax
import jax.numpy as jnp
from jax import lax
from jax.experimental import pallas as pl
from jax.experimental.pallas import tpu as pltpu

F32 = jnp.float32
BF16 = jnp.bfloat16

HEAD_DIM = 64
ATT_HEADS = 6
ATT_KV_HEADS = 2
ATT_GROUP = ATT_HEADS // ATT_KV_HEADS
ATT_DIM = ATT_HEADS * HEAD_DIM
ATT_KV_DIM = ATT_KV_HEADS * HEAD_DIM
WINDOW = 128
N_BUCKETS = 32
MAX_DISTANCE = 128
RWKV_HEADS = 6
RWKV_DIM = RWKV_HEADS * HEAD_DIM
DECAY_LORA = 32
ICLR_LORA = 32
GATE_LORA = 64
LORA_COLS = DECAY_LORA + ICLR_LORA + GATE_LORA
RWKV_COLS = 3 * RWKV_DIM + LORA_COLS
GN_EPS = 64e-5
SSM_GROUPS = 16
SSM_GROUP_CH = 16
SSM_DIM = SSM_GROUPS * SSM_GROUP_CH
SSM_STATE = 64
RMS_EPS = 1e-6
MASK_VALUE = -1e30

VMEM_LIMIT_BYTES = 56 * 1024 * 1024
RWKV_CHUNK = 64
SSM_CHUNK = 16


def _params(*sem):
    return pltpu.CompilerParams(dimension_semantics=sem, vmem_limit_bytes=VMEM_LIMIT_BYTES)


def _rms(x, g):
    return x * lax.rsqrt(jnp.mean(x * x, axis=-1, keepdims=True) + RMS_EPS) * g


def _const_spec(shape):
    nd = len(shape)
    return pl.BlockSpec(shape, lambda *_: (0,) * nd)


def _ffn_kernel(x_ref, gpre_ref, gpost_ref, wg_ref, wu_ref, wd_ref, o_ref, acc_ref, *, ff_chunk, n_chunks):
    x = x_ref[...]
    xn = _rms(x, gpre_ref[...]).astype(BF16)
    for c in range(n_chunks):
        sl = slice(c * ff_chunk, (c + 1) * ff_chunk)
        g = jnp.dot(xn, wg_ref[:, sl], preferred_element_type=F32)
        u = jnp.dot(xn, wu_ref[:, sl], preferred_element_type=F32)
        a = (g * jax.nn.sigmoid(g) * u).astype(BF16)
        d = jnp.dot(a, wd_ref[sl, :], preferred_element_type=F32)
        if c == 0:
            acc_ref[...] = d
        else:
            acc_ref[...] += d
    o_ref[...] = x + 0.5 * _rms(acc_ref[...], gpost_ref[...])


def _ffn_call(h, g_pre, g_post, wg, wu, wd, *, tm, ff_chunk):
    n, d = h.shape
    dff = wg.shape[1]
    kern = functools.partial(_ffn_kernel, ff_chunk=ff_chunk, n_chunks=dff // ff_chunk)
    return pl.pallas_call(
        kern,
        grid=(n // tm,),
        in_specs=[
            pl.BlockSpec((tm, d), lambda i: (i, 0)),
            _const_spec((1, d)), _const_spec((1, d)),
            _const_spec((d, dff)), _const_spec((d, dff)), _const_spec((dff, d)),
        ],
        out_specs=pl.BlockSpec((tm, d), lambda i: (i, 0)),
        out_shape=jax.ShapeDtypeStruct((n, d), F32),
        scratch_shapes=[pltpu.VMEM((tm, d), F32)],
        compiler_params=_params("parallel"),
    )(h, g_pre.reshape(1, d), g_post.reshape(1, d), wg, wu, wd)


_IN_SPLITS = (ATT_DIM, ATT_KV_DIM, ATT_KV_DIM, RWKV_COLS, SSM_DIM)


def _inproj_kernel(x_ref, g_ref, w_ref, q_ref, k_ref, v_ref, rw_ref, su_ref):
    xn = _rms(x_ref[...], g_ref[...]).astype(BF16)
    off = 0
    for ref, width in zip((q_ref, k_ref, v_ref, rw_ref, su_ref), _IN_SPLITS):
        ref[...] = jnp.dot(xn, w_ref[:, off:off + width], preferred_element_type=F32)
        off += width


def _inproj_call(h, g_pre, w_in, *, tm):
    n, d = h.shape
    cols = w_in.shape[1]
    return pl.pallas_call(
        _inproj_kernel,
        grid=(n // tm,),
        in_specs=[pl.BlockSpec((tm, d), lambda i: (i, 0)), _const_spec((1, d)), _const_spec((d, cols))],
        out_specs=[pl.BlockSpec((tm, w), lambda i: (i, 0)) for w in _IN_SPLITS],
        out_shape=[jax.ShapeDtypeStruct((n, w), F32) for w in _IN_SPLITS],
        compiler_params=_params("parallel"),
    )(h, g_pre.reshape(1, d), w_in)


def _band_bias(rel_bias):
    qi = jnp.arange(WINDOW)[:, None]
    kj = jnp.arange(2 * WINDOW)[None, :]
    rel = qi + WINDOW - kj
    in_window = (rel >= 0) & (rel < WINDOW)
    n = jnp.maximum(rel, 0)
    max_exact = N_BUCKETS // 2
    nf = jnp.maximum(n, 1).astype(F32)
    large = max_exact + (jnp.log(nf / max_exact) / math.log(MAX_DISTANCE / max_exact)
                         * (N_BUCKETS - max_exact)).astype(jnp.int32)
    large = jnp.minimum(large, N_BUCKETS - 1)
    bucket = jnp.where(n < max_exact, n, large)
    bias = jnp.transpose(rel_bias.astype(F32)[bucket], (2, 0, 1))
    return jnp.where(in_window[None], bias, MASK_VALUE)


def _attn_kernel(sink_ref, q_ref, kp_ref, kc_ref, vp_ref, vc_ref, bias_ref, o_ref):
    blk = pl.program_id(1)
    q = q_ref[...] * (HEAD_DIM ** -0.5)
    k = jnp.concatenate([kp_ref[...], kc_ref[...]], axis=0).astype(BF16)
    v = jnp.concatenate([vp_ref[...], vc_ref[...]], axis=0).astype(BF16)
    col = lax.broadcasted_iota(jnp.int32, (1, 2 * WINDOW), 1)
    key_exists = jnp.logical_or(blk > 0, col >= WINDOW)
    outs = []
    for j in range(ATT_KV_HEADS):
        kj = k[:, j * HEAD_DIM:(j + 1) * HEAD_DIM]
        vj = v[:, j * HEAD_DIM:(j + 1) * HEAD_DIM]
        for gi in range(ATT_GROUP):
            h = j * ATT_GROUP + gi
            qh = q[:, h * HEAD_DIM:(h + 1) * HEAD_DIM].astype(BF16)
            s = lax.dot_general(qh, kj, (((1,), (1,)), ((), ())), preferred_element_type=F32)
            s = jnp.where(key_exists, s + bias_ref[h], MASK_VALUE)
            sink = sink_ref[h]
            m = jnp.maximum(jnp.max(s, axis=-1, keepdims=True), sink)
            p = jnp.exp(s - m)
            denom = jnp.sum(p, axis=-1, keepdims=True) + jnp.exp(sink - m)
            o = jnp.dot(p.astype(BF16), vj, preferred_element_type=F32)
            outs.append(o / denom)
    o_ref[...] = jnp.concatenate(outs, axis=-1)


def _attn_call(q, k, v, bias, sinks):
    bsz, t, _ = q.shape
    nb = t // WINDOW
    cur = lambda b, n: (b, n, 0)
    prv = lambda b, n: (b, jnp.maximum(n - 1, 0), 0)
    kv_blk = (None, WINDOW, ATT_KV_DIM)
    return pl.pallas_call(
        _attn_kernel,
        grid=(bsz, nb),
        in_specs=[
            pl.BlockSpec(memory_space=pltpu.SMEM),
            pl.BlockSpec((None, WINDOW, ATT_DIM), cur),
            pl.BlockSpec(kv_blk, prv), pl.BlockSpec(kv_blk, cur),
            pl.BlockSpec(kv_blk, prv), pl.BlockSpec(kv_blk, cur),
            _const_spec((ATT_HEADS, WINDOW, 2 * WINDOW)),
        ],
        out_specs=pl.BlockSpec((None, WINDOW, ATT_DIM), cur),
        out_shape=jax.ShapeDtypeStruct((bsz, t, ATT_DIM), F32),
        compiler_params=_params("parallel", "parallel"),
    )(sinks, q, k, k, v, v, bias)


RWKV_PAIR = 2
RWKV_GROUP_HEADS = 4
RWKV_GROUP_W = RWKV_GROUP_HEADS * HEAD_DIM
RWKV_GROUPS = RWKV_PAIR * RWKV_HEADS // RWKV_GROUP_HEADS


def _gdot(a, b, contract):
    return lax.dot_general(a.astype(BF16), b.astype(BF16), (contract, ((0,), (0,))),
                           preferred_element_type=F32)


def _gmm(a, b):
    return _gdot(a, b, ((2,), (1,)))


def _gmm_nt(a, b):
    return _gdot(a, b, ((2,), (2,)))


def _gmm_tn(a, b):
    return _gdot(a, b, ((1,), (1,)))


def _flat_groups(x):
    return jnp.stack([x[:, j * RWKV_GROUP_W:(j + 1) * RWKV_GROUP_W] for j in range(RWKV_GROUPS)], axis=0)


def _head_rows(x):
    lane_head = lax.broadcasted_iota(jnp.int32, (1, RWKV_GROUP_W), 1) // HEAD_DIM
    zero = jnp.zeros((), x.dtype)
    groups = []
    for j in range(RWKV_GROUPS):
        xg = x[:, j * RWKV_GROUP_W:(j + 1) * RWKV_GROUP_W]
        groups.append(jnp.concatenate([jnp.where(lane_head == i, xg, zero) for i in range(RWKV_GROUP_HEADS)], axis=0))
    return jnp.stack(groups, axis=0)


def _rwkv_kernel(p_ref, mu_ref, w0_ref, a0_ref, kk_ref, ka_ref, rk_ref, gnw_ref, gnb_ref,
                 lora_ref, hb_ref, tril_ref, o_ref, prev_ref, s_ref, *, n_chunks):
    C, D, W = RWKV_CHUNK, RWKV_DIM, RWKV_GROUP_W
    tt = n_chunks * C

    @pl.when(pl.program_id(1) == 0)
    def _():
        prev_ref[...] = jnp.zeros_like(prev_ref)
        s_ref[...] = jnp.zeros_like(s_ref)

    p = p_ref[...].reshape(RWKV_PAIR * tt, RWKV_COLS)
    row = lax.broadcasted_iota(jnp.int32, p.shape, 0)
    prev = pltpu.roll(p, 1, 0)
    for i in range(RWKV_PAIR):
        prev = jnp.where(row == i * tt, prev_ref[i], prev)
        prev_ref[i] = p[(i + 1) * tt - 1:(i + 1) * tt, :]
    xs = p + (prev - p) * mu_ref[...]
    r, k, v = xs[:, 0:D], xs[:, D:2 * D], xs[:, 2 * D:3 * D]
    lo = xs[:, 3 * D:3 * D + LORA_COLS]
    lane = lax.broadcasted_iota(jnp.int32, lo.shape, 1)
    act = jnp.where(lane < DECAY_LORA, jnp.tanh(lo),
                    jnp.where(lane < DECAY_LORA + ICLR_LORA, lo, jax.nn.sigmoid(lo)))
    proj = jnp.dot(act.astype(BF16), lora_ref[...], preferred_element_type=F32)
    z = -(w0_ref[...] + proj[:, 0:D])
    softplus = jnp.maximum(z, 0.0) + jnp.log(1.0 + jnp.exp(-jnp.abs(z)))
    logw = -jnp.exp(-softplus - 0.5)
    a = jax.nn.sigmoid(a0_ref[...] + proj[:, D:2 * D])
    g = proj[:, 2 * D:3 * D]
    hb = hb_ref[...]
    kk = k * kk_ref[...]
    ss = jnp.dot((kk * kk).astype(BF16), hb, preferred_element_type=F32)
    kk = kk / jnp.maximum(jnp.sqrt(ss), 1e-12)
    k = k * (1.0 + (a - 1.0) * ka_ref[...])

    cum = jnp.dot(tril_ref[...], logw, precision=lax.Precision.HIGHEST, preferred_element_type=F32)
    e_pos = jnp.exp(cum)
    e_neg = jnp.exp(-cum)
    kb = kk * a
    r_t = (r * e_pos).astype(BF16)
    k_t = (k * e_neg).astype(BF16)
    b_t = (kb * e_neg).astype(BF16)
    a_t = (-kk * jnp.exp(cum - logw)).astype(BF16)
    v_b = v.astype(BF16)

    r64 = lax.broadcasted_iota(jnp.int32, (RWKV_GROUP_HEADS * C, W), 0) % C
    c64 = lax.broadcasted_iota(jnp.int32, (RWKV_GROUP_HEADS * C, W), 1) % C
    strict = (r64 > c64)[None]
    eye = (lax.broadcasted_iota(jnp.int32, (RWKV_GROUP_HEADS * C, W), 0)
           == lax.broadcasted_iota(jnp.int32, (RWKV_GROUP_HEADS * C, W), 1))[None].astype(F32)
    incl = (lax.broadcasted_iota(jnp.int32, (C, W), 0) >= lax.broadcasted_iota(jnp.int32, (C, W), 1) % C)[None]

    def pair_lanes(x, c):
        return jnp.concatenate([x[i * tt + c * C:i * tt + (c + 1) * C] for i in range(RWKV_PAIR)], axis=1)

    ys = []
    for c in range(n_chunks):
        g_end = pair_lanes(e_pos, c)[C - 1:C, :]
        k_end = (pair_lanes(k * e_neg, c) * g_end).astype(BF16)
        b_end = (pair_lanes(kb * e_neg, c) * g_end).astype(BF16)
        ax, bx, kx = _head_rows(pair_lanes(a_t, c)), _head_rows(pair_lanes(b_t, c)), _head_rows(pair_lanes(k_t, c))
        vx = _head_rows(pair_lanes(v_b, c))
        rf = _flat_groups(pair_lanes(r_t, c))
        sc = _gmm_nt(jnp.concatenate([ax, rf], axis=1), jnp.concatenate([bx, kx], axis=1))
        a_ab = jnp.where(strict, sc[:, :4 * C, :W], 0.0)
        a_ak = jnp.where(strict, sc[:, :4 * C, W:], 0.0)
        a_rb = jnp.where(incl, sc[:, 4 * C:, :W], 0.0)
        a_rk = jnp.where(incl, sc[:, 4 * C:, W:], 0.0)
        inv = eye + a_ab
        powr = a_ab
        span = 1
        while span * 2 < C:
            powr = _gmm(powr, powr)
            inv = inv + _gmm(powr, inv)
            span *= 2
        akv = _gmm(a_ak, vx)
        hat = _gmm(inv, jnp.concatenate([ax, akv.astype(BF16)], axis=-1))
        a_hat, v_hat = hat[..., :W], hat[..., W:]

        s = s_ref[...]
        sr = _gmm_nt(jnp.concatenate([a_hat.astype(BF16), rf], axis=1), s)
        u = sr[:, :4 * C] + v_hat
        uv = jnp.concatenate([u.astype(BF16), vx], axis=1)
        y = sr[:, 4 * C:] + _gmm(jnp.concatenate([a_rb, a_rk], axis=-1), uv)
        s_ref[...] = (s * _flat_groups(g_end)
                      + _gmm_tn(uv, jnp.concatenate([_head_rows(b_end), _head_rows(k_end)], axis=1)))
        ys.append(jnp.concatenate([y[j] for j in range(RWKV_GROUPS)], axis=-1))

    y = jnp.concatenate([yc[:, i * D:(i + 1) * D] for i in range(RWKV_PAIR) for yc in ys], axis=0)
    inv_n = 1.0 / HEAD_DIM
    mean = jnp.dot(y.astype(BF16), hb, preferred_element_type=F32) * inv_n
    yc = y - mean
    var = jnp.dot((yc * yc).astype(BF16), hb, preferred_element_type=F32) * inv_n
    y = yc * lax.rsqrt(var + GN_EPS) * gnw_ref[...] + gnb_ref[...]
    rk = jnp.dot((r * k * rk_ref[...]).astype(BF16), hb, preferred_element_type=F32)
    o_ref[...] = ((y + rk * v) * g).reshape(RWKV_PAIR, tt, D)


def _rwkv_call(rw, mu, w0, w2, a0, a2, g2, k_k, k_a, r_k, gn_w, gn_b, *, n_chunks):
    bsz, t, cols = rw.shape
    C, D = RWKV_CHUNK, RWKV_DIM
    tt = n_chunks * C
    lora = jnp.zeros((LORA_COLS, 3 * D), F32)
    lora = lora.at[0:DECAY_LORA, 0:D].set(w2)
    lora = lora.at[DECAY_LORA:DECAY_LORA + ICLR_LORA, D:2 * D].set(a2)
    lora = lora.at[DECAY_LORA + ICLR_LORA:, 2 * D:].set(g2)
    head_id = jnp.arange(D) // HEAD_DIM
    hb = (head_id[:, None] == head_id[None, :]).astype(BF16)
    pos = jnp.arange(RWKV_PAIR * tt)
    tril = ((pos[:, None] >= pos[None, :]) & (pos[:, None] // C == pos[None, :] // C)).astype(F32)
    vec = lambda x: x.reshape(1, -1).astype(F32)
    vecs = [vec(mu), vec(w0), vec(a0), vec(k_k), vec(k_a), vec(r_k), vec(gn_w), vec(gn_b)]
    consts = vecs + [lora.astype(BF16), hb, tril]
    kern = functools.partial(_rwkv_kernel, n_chunks=n_chunks)
    return pl.pallas_call(
        kern,
        grid=(bsz // RWKV_PAIR, t // tt),
        in_specs=[pl.BlockSpec((RWKV_PAIR, tt, cols), lambda b, c: (b, c, 0))] + [_const_spec(x.shape) for x in consts],
        out_specs=pl.BlockSpec((RWKV_PAIR, tt, D), lambda b, c: (b, c, 0)),
        out_shape=jax.ShapeDtypeStruct((bsz, t, D), F32),
        scratch_shapes=[pltpu.VMEM((RWKV_PAIR, 1, cols), F32),
                        pltpu.VMEM((RWKV_GROUPS, RWKV_GROUP_W, RWKV_GROUP_W), F32)],
        compiler_params=_params("parallel", "arbitrary"),
    )(rw, *consts)


SSM_TILE_GROUPS = 2
SSM_TILES = SSM_GROUPS // SSM_TILE_GROUPS
SSM_HALF = SSM_TILE_GROUPS * SSM_STATE
SSM_LANE_TILE = 128


def _s5_operators(lam_re, lam_im, log_dt, b_re, b_im, c_re, c_im, n_levels):
    L, G, P, H = SSM_CHUNK, SSM_GROUPS, SSM_STATE, SSM_GROUP_CH
    hi = lax.Precision.HIGHEST
    lr = jnp.minimum(lam_re.astype(F32), -1e-4)
    li = lam_im.astype(F32)
    dt = jnp.exp(log_dt.astype(F32))[:, None]
    n = jnp.arange(L + 1, dtype=F32)[:, None, None]
    mag = jnp.exp(lr * dt * n)
    pr = mag * jnp.cos(li * dt * n)
    pi = mag * jnp.sin(li * dt * n)
    den = lr * lr + li * li
    nr, ni = pr[1] - 1.0, pi[1]
    fr = (nr * lr + ni * li) / den
    fi = (ni * lr - nr * li) / den
    b_re, b_im = b_re.astype(F32), b_im.astype(F32)
    bb_re = fr[..., None] * b_re - fi[..., None] * b_im
    bb_im = fr[..., None] * b_im + fi[..., None] * b_re
    c_re, c_im = c_re.astype(F32), c_im.astype(F32)
    cl_re = c_re[None] * pr[:L, :, None, :] - c_im[None] * pi[:L, :, None, :]
    cl_im = c_re[None] * pi[:L, :, None, :] + c_im[None] * pr[:L, :, None, :]
    kern = (jnp.einsum('nghp,gpk->nghk', cl_re, bb_re, precision=hi)
            - jnp.einsum('nghp,gpk->nghk', cl_im, bb_im, precision=hi))
    eye = jnp.eye(G, dtype=F32)
    k_lag = jnp.einsum('lgab,gk->lgbka', kern, eye).reshape(L, G * H, G * H)
    b_full = lambda bb: jnp.einsum('gph,gk->ghkp', bb, eye).reshape(G * H, G * P)
    c_full = lambda cc: jnp.einsum('ghp,gk->gpkh', cc, eye).reshape(G * P, G * H)
    tile_cols = lambda m: m.reshape(G * H, SSM_TILES, SSM_HALF).transpose(1, 0, 2)
    b_cat = jnp.concatenate([tile_cols(b_full(bb_re)), tile_cols(b_full(bb_im))], axis=-1)
    tile_rows = lambda m: m.reshape(SSM_TILES, SSM_HALF, G * H)
    c_cat = jnp.concatenate([tile_rows(c_full(c_re)), -tile_rows(c_full(c_im))], axis=1)
    lam = lambda x: x.reshape(SSM_TILES, 1, SSM_HALF)
    qr, qi = pr[L], pi[L]
    scan_r, scan_i = [], []
    for _ in range(n_levels):
        scan_r.append(lam(qr))
        scan_i.append(lam(qi))
        qr, qi = qr * qr - qi * qi, 2.0 * qr * qi
    return (k_lag.astype(BF16), b_cat.astype(BF16), c_cat.astype(BF16), lam(pr[1]), lam(pi[1]),
            jnp.stack(scan_r), jnp.stack(scan_i))


def _cmul(x, cr, ci):
    re, im = x[:, :SSM_HALF], x[:, SSM_HALF:]
    return jnp.concatenate([cr * re - ci * im, cr * im + ci * re], axis=1)


def _s5_kernel(u_ref, klag_ref, bcat_ref, ccat_ref, stepr_ref, stepi_ref, scanr_ref, scani_ref, d_ref,
               gluw_ref, glub_ref, o_ref, ut_ref, utb_ref, acc_ref, p_ref, z_ref, *, n_chunks, n_levels):
    L, nc = SSM_CHUNK, n_chunks
    lanes = u_ref.shape[-1]
    nt = SSM_DIM // lanes
    for s in range(L):
        rows = jnp.concatenate([u_ref[pl.ds(s * nt + i, nc, stride=L * nt), :] for i in range(nt)], axis=1)
        ut_ref[s * nc:(s + 1) * nc, :] = rows
        utb_ref[s * nc:(s + 1) * nc, :] = rows.astype(BF16)
    acc_ref[...] = d_ref[...] * ut_ref[...]
    for lag in range(L):
        acc_ref[lag * nc:, :] += jnp.dot(utb_ref[0:(L - lag) * nc, :], klag_ref[lag], preferred_element_type=F32)
    row = lax.broadcasted_iota(jnp.int32, (nc, 2 * SSM_HALF), 0)

    def tile(j, carry):
        cr, ci = stepr_ref[j], stepi_ref[j]
        p_ref[...] = jnp.dot(utb_ref[...], bcat_ref[j], preferred_element_type=F32)
        x = p_ref[0:nc, :]
        for s in range(1, L):
            x = _cmul(x, cr, ci) + p_ref[s * nc:(s + 1) * nc, :]
        for k in range(n_levels):
            sh = jnp.where(row >= (1 << k), pltpu.roll(x, 1 << k, 0), 0.0)
            x = x + _cmul(sh, scanr_ref[k, j], scani_ref[k, j])
        z = jnp.where(row >= 1, pltpu.roll(x, 1, 0), 0.0)
        for t in range(L):
            z = _cmul(z, cr, ci)
            z_ref[t * nc:(t + 1) * nc, :] = z.astype(BF16)
        acc_ref[...] += jnp.dot(z_ref[...], ccat_ref[j], preferred_element_type=F32)
        return carry

    lax.fori_loop(0, SSM_TILES, tile, 0)
    y = jax.nn.gelu(acc_ref[...])
    gate = jnp.dot(y.astype(BF16), gluw_ref[...], preferred_element_type=F32) + glub_ref[...]
    acc_ref[...] = y * jax.nn.sigmoid(gate)
    for t in range(L):
        for i in range(nt):
            o_ref[pl.ds(t * nt + i, nc, stride=L * nt), :] = acc_ref[t * nc:(t + 1) * nc, i * lanes:(i + 1) * lanes]


def _s5_call(su, ops, d, glu_w, glu_b):
    bsz, t, width = su.shape
    nc = t // SSM_CHUNK
    n_levels = ops[-1].shape[0]
    consts = list(ops) + [d.reshape(1, width).astype(F32), glu_w, glu_b.reshape(1, width).astype(F32)]
    kern = functools.partial(_s5_kernel, n_chunks=nc, n_levels=n_levels)
    nt = width // SSM_LANE_TILE
    seq = pl.BlockSpec((None, t * nt, SSM_LANE_TILE), lambda b: (b, 0, 0))
    out = pl.pallas_call(
        kern,
        grid=(bsz,),
        in_specs=[seq] + [_const_spec(x.shape) for x in consts],
        out_specs=seq,
        out_shape=jax.ShapeDtypeStruct((bsz, t * nt, SSM_LANE_TILE), F32),
        scratch_shapes=[pltpu.VMEM((t, width), F32), pltpu.VMEM((t, width), BF16), pltpu.VMEM((t, width), F32),
                        pltpu.VMEM((t, 2 * SSM_HALF), F32), pltpu.VMEM((t, 2 * SSM_HALF), BF16)],
        compiler_params=_params("parallel"),
    )(su.reshape(bsz, t * nt, SSM_LANE_TILE), *consts)
    return out.reshape(bsz, t, width)


def _outproj_kernel(h_ref, ya_ref, yr_ref, ys_ref, wout_ref, g_ref, o_ref):
    y = jnp.concatenate([ya_ref[...], yr_ref[...], ys_ref[...]], axis=-1).astype(BF16)
    m = jnp.dot(y, wout_ref[...], preferred_element_type=F32)
    o_ref[...] = h_ref[...] + _rms(m, g_ref[...])


def _outproj_call(h, ya, yr, ys, w_out, g_post, *, tm):
    n, d = h.shape
    row = lambda w: pl.BlockSpec((tm, w), lambda i: (i, 0))
    return pl.pallas_call(
        _outproj_kernel,
        grid=(n // tm,),
        in_specs=[row(d), row(ATT_DIM), row(RWKV_DIM), row(SSM_DIM), _const_spec(w_out.shape), _const_spec((1, d))],
        out_specs=row(d),
        out_shape=jax.ShapeDtypeStruct((n, d), F32),
        compiler_params=_params("parallel"),
    )(h, ya, yr, ys, w_out, g_post.reshape(1, d))


def kernel(x, rel_bias, ln_pre_ffn1, ln_post_ffn1, ffn1_w_gate, ffn1_w_up, ffn1_w_down, ln_pre_mix, ln_post_mix, w_in, w_out, att_sinks, rwkv_mu, rwkv_w0, rwkv_w2, rwkv_a0, rwkv_a2, rwkv_g2, rwkv_k_k, rwkv_k_a, rwkv_r_k, rwkv_gn_w, rwkv_gn_b, ssm_lambda_re, ssm_lambda_im, ssm_log_dt, ssm_b_re, ssm_b_im, ssm_c_re, ssm_c_im, ssm_d, ssm_glu_w, ssm_glu_b, ln_pre_ffn2, ln_post_ffn2, ffn2_w_gate, ffn2_w_up, ffn2_w_down):
    bsz, t, d = x.shape
    n = bsz * t
    depth = w_in.shape[0]
    tm = 512 if n % 512 == 0 else 128
    ff_chunk = 256
    bf = lambda w: w.astype(BF16)
    bias = _band_bias(rel_bias)
    h = x.reshape(n, d)
    for l in range(depth):
        h = _ffn_call(h, ln_pre_ffn1[l], ln_post_ffn1[l], bf(ffn1_w_gate[l]), bf(ffn1_w_up[l]), bf(ffn1_w_down[l]),
                      tm=tm, ff_chunk=ff_chunk)
        q, k, v, rw, su = _inproj_call(h, ln_pre_mix[l], bf(w_in[l]), tm=tm)
        seq = lambda z: z.reshape(bsz, t, z.shape[-1])
        ya = _attn_call(seq(q), seq(k), seq(v), bias, att_sinks[l].astype(F32))
        yr = _rwkv_call(seq(rw), rwkv_mu[l], rwkv_w0[l], rwkv_w2[l], rwkv_a0[l], rwkv_a2[l], rwkv_g2[l],
                        rwkv_k_k[l], rwkv_k_a[l], rwkv_r_k[l], rwkv_gn_w[l], rwkv_gn_b[l], n_chunks=1)
        ops = _s5_operators(ssm_lambda_re[l], ssm_lambda_im[l], ssm_log_dt[l], ssm_b_re[l], ssm_b_im[l],
                            ssm_c_re[l], ssm_c_im[l], n_levels=max(1, math.ceil(math.log2(t // SSM_CHUNK))))
        ys = _s5_call(seq(su), ops, ssm_d[l], bf(ssm_glu_w[l]), ssm_glu_b[l])
        h = _outproj_call(h, ya.reshape(n, -1), yr.reshape(n, -1), ys.reshape(n, -1), bf(w_out[l]), ln_post_mix[l],
                          tm=tm)
        h = _ffn_call(h, ln_pre_ffn2[l], ln_post_ffn2[l], bf(ffn2_w_gate[l]), bf(ffn2_w_up[l]), bf(ffn2_w_down[l]),
                      tm=tm, ff_chunk=ff_chunk)
    return h.reshape(bsz, t, d)
```

```python
import functools
import math

import jax
import jax.numpy as jnp
from jax import lax
from jax.experimental import pallas as pl
from jax.experimental.pallas import tpu as pltpu

F32 = jnp.float32
BF16 = jnp.bfloat16

HEAD_DIM = 64
ATT_HEADS = 6
ATT_KV_HEADS = 2
ATT_GROUP = ATT_HEADS // ATT_KV_HEADS
ATT_DIM = ATT_HEADS * HEAD_DIM
ATT_KV_DIM = ATT_KV_HEADS * HEAD_DIM
WINDOW = 128
N_BUCKETS = 32
MAX_DISTANCE = 128
RWKV_HEADS = 6
RWKV_DIM = RWKV_HEADS * HEAD_DIM
DECAY_LORA = 32
ICLR_LORA = 32
GATE_LORA = 64
LORA_COLS = DECAY_LORA + ICLR_LORA + GATE_LORA
RWKV_COLS = 3 * RWKV_DIM + LORA_COLS
GN_EPS = 64e-5
SSM_GROUPS = 16
SSM_GROUP_CH = 16
SSM_DIM = SSM_GROUPS * SSM_GROUP_CH
SSM_STATE = 64
RMS_EPS = 1e-6
MASK_VALUE = -1e30

VMEM_LIMIT_BYTES = 56 * 1024 * 1024
RWKV_CHUNK = 64
SSM_CHUNK = 16


def _params(*sem):
    return pltpu.CompilerParams(dimension_semantics=sem, vmem_limit_bytes=VMEM_LIMIT_BYTES)


def _rms(x, g):
    return x * lax.rsqrt(jnp.mean(x * x, axis=-1, keepdims=True) + RMS_EPS) * g


def _const_spec(shape):
    nd = len(shape)
    return pl.BlockSpec(shape, lambda *_: (0,) * nd)


def _ffn_kernel(x_ref, gpre_ref, gpost_ref, wg_ref, wu_ref, wd_ref, o_ref, acc_ref, *, ff_chunk, n_chunks):
    x = x_ref[...]
    xn = _rms(x, gpre_ref[...]).astype(BF16)
    for c in range(n_chunks):
        sl = slice(c * ff_chunk, (c + 1) * ff_chunk)
        g = jnp.dot(xn, wg_ref[:, sl], preferred_element_type=F32)
        u = jnp.dot(xn, wu_ref[:, sl], preferred_element_type=F32)
        a = (g * jax.nn.sigmoid(g) * u).astype(BF16)
        d = jnp.dot(a, wd_ref[sl, :], preferred_element_type=F32)
        if c == 0:
            acc_ref[...] = d
        else:
            acc_ref[...] += d
    o_ref[...] = x + 0.5 * _rms(acc_ref[...], gpost_ref[...])


def _ffn_call(h, g_pre, g_post, wg, wu, wd, *, tm, ff_chunk):
    n, d = h.shape
    dff = wg.shape[1]
    kern = functools.partial(_ffn_kernel, ff_chunk=ff_chunk, n_chunks=dff // ff_chunk)
    return pl.pallas_call(
        kern,
        grid=(n // tm,),
        in_specs=[
            pl.BlockSpec((tm, d), lambda i: (i, 0)),
            _const_spec((1, d)), _const_spec((1, d)),
            _const_spec((d, dff)), _const_spec((d, dff)), _const_spec((dff, d)),
        ],
        out_specs=pl.BlockSpec((tm, d), lambda i: (i, 0)),
        out_shape=jax.ShapeDtypeStruct((n, d), F32),
        scratch_shapes=[pltpu.VMEM((tm, d), F32)],
        compiler_params=_params("parallel"),
    )(h, g_pre.reshape(1, d), g_post.reshape(1, d), wg, wu, wd)


_IN_SPLITS = ((ATT_DIM, BF16), (ATT_KV_DIM, BF16), (ATT_KV_DIM, BF16), (RWKV_COLS, F32),
              (SSM_DIM // 2, F32), (SSM_DIM // 2, F32))


def _inproj_kernel(x_ref, g_ref, w_ref, *out_refs):
    xn = _rms(x_ref[...], g_ref[...]).astype(BF16)
    off = 0
    for ref, (width, dtype) in zip(out_refs, _IN_SPLITS):
        ref[...] = jnp.dot(xn, w_ref[:, off:off + width], preferred_element_type=F32).astype(dtype)
        off += width


def _inproj_call(h, g_pre, w_in, *, tm):
    n, d = h.shape
    cols = w_in.shape[1]
    return pl.pallas_call(
        _inproj_kernel,
        grid=(n // tm,),
        in_specs=[pl.BlockSpec((tm, d), lambda i: (i, 0)), _const_spec((1, d)), _const_spec((d, cols))],
        out_specs=[pl.BlockSpec((tm, w), lambda i: (i, 0)) for w, _ in _IN_SPLITS],
        out_shape=[jax.ShapeDtypeStruct((n, w), dt) for w, dt in _IN_SPLITS],
        compiler_params=_params("parallel"),
    )(h, g_pre.reshape(1, d), w_in)


ATT_BLOCKS = 2


def _band_bias(rel_bias):
    qi = jnp.arange(WINDOW)[:, None]
    kj = jnp.arange(2 * WINDOW)[None, :]
    rel = qi + WINDOW - kj
    in_window = (rel >= 0) & (rel < WINDOW)
    n = jnp.maximum(rel, 0)
    max_exact = N_BUCKETS // 2
    nf = jnp.maximum(n, 1).astype(F32)
    large = max_exact + (jnp.log(nf / max_exact) / math.log(MAX_DISTANCE / max_exact)
                         * (N_BUCKETS - max_exact)).astype(jnp.int32)
    large = jnp.minimum(large, N_BUCKETS - 1)
    bucket = jnp.where(n < max_exact, n, large)
    onehot = (bucket[..., None] == jnp.arange(N_BUCKETS)).astype(F32)
    bias = jnp.einsum('qkb,bh->hqk', onehot, rel_bias.astype(F32), precision=lax.Precision.HIGHEST)
    return jnp.transpose(jnp.where(in_window[None], bias, MASK_VALUE), (0, 2, 1))


def _attn_kernel(sink_ref, q_ref, kp_ref, kc_ref, vp_ref, vc_ref, bias_ref, o_ref):
    first = pl.program_id(1) == 0
    q = q_ref[...] * (HEAD_DIM ** -0.5)
    k = jnp.concatenate([kp_ref[...], kc_ref[...]], axis=0)
    v = jnp.concatenate([vp_ref[...], vc_ref[...]], axis=0)
    row = lax.broadcasted_iota(jnp.int32, (1, 2 * WINDOW, 1), 1)
    has_prev = jnp.logical_or(jnp.logical_not(first), row >= WINDOW)
    sink = sink_ref[...]
    kv_of = [h // ATT_GROUP for h in range(ATT_HEADS)]
    head = lambda x, j: x[:, j * HEAD_DIM:(j + 1) * HEAD_DIM]
    blocks = []
    for i in range(ATT_BLOCKS):
        band = slice(i * WINDOW, (i + 2) * WINDOW)
        qi = q[i * WINDOW:(i + 1) * WINDOW]
        v_t = v[band].astype(F32).T.astype(BF16)
        qs = jnp.stack([head(qi, h) for h in range(ATT_HEADS)], axis=0)
        ks = jnp.stack([head(k[band], j) for j in kv_of], axis=0)
        vs = jnp.stack([v_t[j * HEAD_DIM:(j + 1) * HEAD_DIM] for j in kv_of], axis=0)
        s = lax.dot_general(ks, qs, (((2,), (2,)), ((0,), (0,))), preferred_element_type=F32)
        s = s + bias_ref[...]
        if i == 0:
            s = jnp.where(has_prev, s, MASK_VALUE)
        m = jnp.maximum(jnp.max(s, axis=1, keepdims=True), sink)
        p = jnp.exp(s - m)
        denom = jnp.sum(p, axis=1, keepdims=True) + jnp.exp(sink - m)
        o = lax.dot_general(vs, p.astype(BF16), (((2,), (1,)), ((0,), (0,))), preferred_element_type=F32)
        o = o / denom
        blocks.append(o.reshape(ATT_DIM, WINDOW).T)
    o_ref[...] = jnp.concatenate(blocks, axis=0).astype(o_ref.dtype)


def _attn_call(q, k, v, bias, sinks):
    bsz, t, _ = q.shape
    span = ATT_BLOCKS * WINDOW
    cur = lambda b, n: (b, n, 0)
    prv = lambda b, n: (b, jnp.maximum(n * ATT_BLOCKS - 1, 0), 0)
    cur_kv = pl.BlockSpec((None, span, ATT_KV_DIM), cur)
    prv_kv = pl.BlockSpec((None, WINDOW, ATT_KV_DIM), prv)
    return pl.pallas_call(
        _attn_kernel,
        grid=(bsz, t // span),
        in_specs=[
            _const_spec((ATT_HEADS, 1, 1)),
            pl.BlockSpec((None, span, ATT_DIM), cur),
            prv_kv, cur_kv, prv_kv, cur_kv,
            _const_spec((ATT_HEADS, 2 * WINDOW, WINDOW)),
        ],
        out_specs=pl.BlockSpec((None, span, ATT_DIM), cur),
        out_shape=jax.ShapeDtypeStruct((bsz, t, ATT_DIM), BF16),
        compiler_params=_params("parallel", "parallel"),
    )(sinks.reshape(ATT_HEADS, 1, 1), q, k, k, v, v, bias)


RWKV_PAIR = 2
RWKV_GROUP_HEADS = 4
RWKV_GROUP_W = RWKV_GROUP_HEADS * HEAD_DIM
RWKV_GROUPS = RWKV_PAIR * RWKV_HEADS // RWKV_GROUP_HEADS


def _gdot(a, b, contract):
    return lax.dot_general(a.astype(BF16), b.astype(BF16), (contract, ((0,), (0,))),
                           preferred_element_type=F32)


def _gmm(a, b):
    return _gdot(a, b, ((2,), (1,)))


def _gmm_nt(a, b):
    return _gdot(a, b, ((2,), (2,)))


def _gmm_tn(a, b):
    return _gdot(a, b, ((1,), (1,)))


def _flat_groups(x):
    return jnp.stack([x[:, j * RWKV_GROUP_W:(j + 1) * RWKV_GROUP_W] for j in range(RWKV_GROUPS)], axis=0)


def _head_rows(x):
    lane_head = lax.broadcasted_iota(jnp.int32, (1, RWKV_GROUP_W), 1) // HEAD_DIM
    zero = jnp.zeros((), x.dtype)
    groups = []
    for j in range(RWKV_GROUPS):
        xg = x[:, j * RWKV_GROUP_W:(j + 1) * RWKV_GROUP_W]
        groups.append(jnp.concatenate([jnp.where(lane_head == i, xg, zero) for i in range(RWKV_GROUP_HEADS)], axis=0))
    return jnp.stack(groups, axis=0)


def _rwkv_kernel(p_ref, mu_ref, w0_ref, a0_ref, kk_ref, ka_ref, rk_ref, gnw_ref, gnb_ref,
                 lora_ref, hb_ref, tril_ref, o_ref, prev_ref, s_ref, *, n_chunks):
    C, D, W = RWKV_CHUNK, RWKV_DIM, RWKV_GROUP_W
    tt = n_chunks * C

    @pl.when(pl.program_id(1) == 0)
    def _():
        prev_ref[...] = jnp.zeros_like(prev_ref)
        s_ref[...] = jnp.zeros_like(s_ref)

    p = p_ref[...].reshape(RWKV_PAIR * tt, RWKV_COLS)
    row = lax.broadcasted_iota(jnp.int32, p.shape, 0)
    prev = pltpu.roll(p, 1, 0)
    for i in range(RWKV_PAIR):
        prev = jnp.where(row == i * tt, prev_ref[i], prev)
        prev_ref[i] = p[(i + 1) * tt - 1:(i + 1) * tt, :]
    xs = p + (prev - p) * mu_ref[...]
    r, k, v = xs[:, 0:D], xs[:, D:2 * D], xs[:, 2 * D:3 * D]
    lo = xs[:, 3 * D:3 * D + LORA_COLS]
    lane = lax.broadcasted_iota(jnp.int32, lo.shape, 1)
    act = jnp.where(lane < DECAY_LORA, jnp.tanh(lo),
                    jnp.where(lane < DECAY_LORA + ICLR_LORA, lo, jax.nn.sigmoid(lo)))
    proj = jnp.dot(act.astype(BF16), lora_ref[...], preferred_element_type=F32)
    z = -(w0_ref[...] + proj[:, 0:D])
    softplus = jnp.maximum(z, 0.0) + jnp.log(1.0 + jnp.exp(-jnp.abs(z)))
    logw = -jnp.exp(-softplus - 0.5)
    a = jax.nn.sigmoid(a0_ref[...] + proj[:, D:2 * D])
    g = proj[:, 2 * D:3 * D]
    hb = hb_ref[...]
    kk = k * kk_ref[...]
    ss = jnp.dot((kk * kk).astype(BF16), hb, preferred_element_type=F32)
    kk = kk / jnp.maximum(jnp.sqrt(ss), 1e-12)
    k = k * (1.0 + (a - 1.0) * ka_ref[...])

    logw_hi = logw.astype(BF16)
    logw_lo = (logw - logw_hi.astype(F32)).astype(BF16)
    cum = (jnp.dot(tril_ref[...], logw_hi, preferred_element_type=F32)
           + jnp.dot(tril_ref[...], logw_lo, preferred_element_type=F32))
    e_pos = jnp.exp(cum)
    e_neg = jnp.exp(-cum)
    kb = kk * a
    r_t = (r * e_pos).astype(BF16)
    k_t = (k * e_neg).astype(BF16)
    b_t = (kb * e_neg).astype(BF16)
    a_t = (-kk * jnp.exp(cum - logw)).astype(BF16)
    v_b = v.astype(BF16)

    r64 = lax.broadcasted_iota(jnp.int32, (RWKV_GROUP_HEADS * C, W), 0) % C
    c64 = lax.broadcasted_iota(jnp.int32, (RWKV_GROUP_HEADS * C, W), 1) % C
    strict = (r64 > c64)[None]
    eye = (lax.broadcasted_iota(jnp.int32, (RWKV_GROUP_HEADS * C, W), 0)
           == lax.broadcasted_iota(jnp.int32, (RWKV_GROUP_HEADS * C, W), 1))[None].astype(F32)
    incl = (lax.broadcasted_iota(jnp.int32, (C, W), 0) >= lax.broadcasted_iota(jnp.int32, (C, W), 1) % C)[None]

    def pair_lanes(x, c):
        return jnp.concatenate([x[i * tt + c * C:i * tt + (c + 1) * C] for i in range(RWKV_PAIR)], axis=1)

    ys = []
    for c in range(n_chunks):
        g_end = pair_lanes(e_pos, c)[C - 1:C, :]
        k_end = (pair_lanes(k * e_neg, c) * g_end).astype(BF16)
        b_end = (pair_lanes(kb * e_neg, c) * g_end).astype(BF16)
        ax, bx, kx = _head_rows(pair_lanes(a_t, c)), _head_rows(pair_lanes(b_t, c)), _head_rows(pair_lanes(k_t, c))
        vx = _head_rows(pair_lanes(v_b, c))
        rf = _flat_groups(pair_lanes(r_t, c))
        sc = _gmm_nt(jnp.concatenate([ax, rf], axis=1), jnp.concatenate([bx, kx], axis=1))
        a_ab = jnp.where(strict, sc[:, :4 * C, :W], 0.0)
        a_ak = jnp.where(strict, sc[:, :4 * C, W:], 0.0)
        a_rb = jnp.where(incl, sc[:, 4 * C:, :W], 0.0)
        a_rk = jnp.where(incl, sc[:, 4 * C:, W:], 0.0)
        inv = eye + a_ab
        powr = _gmm(a_ab, a_ab)
        span = 2
        while span < C:
            if 2 * span < C:
                both = _gmm(powr, jnp.concatenate([inv.astype(BF16), powr.astype(BF16)], axis=-1))
                inv, powr = inv + both[..., :W], both[..., W:]
            else:
                inv = inv + _gmm(powr, inv)
            span *= 2
        akv = _gmm(a_ak, vx)
        hat = _gmm(inv, jnp.concatenate([ax, akv.astype(BF16)], axis=-1))
        a_hat, v_hat = hat[..., :W], hat[..., W:]

        s = s_ref[...]
        sr = _gmm_nt(jnp.concatenate([a_hat.astype(BF16), rf], axis=1), s)
        u = sr[:, :4 * C] + v_hat
        uv = jnp.concatenate([u.astype(BF16), vx], axis=1)
        y = sr[:, 4 * C:] + _gmm(jnp.concatenate([a_rb, a_rk], axis=-1), uv)
        s_ref[...] = (s * _flat_groups(g_end)
                      + _gmm_tn(uv, jnp.concatenate([_head_rows(b_end), _head_rows(k_end)], axis=1)))
        ys.append(jnp.concatenate([y[j] for j in range(RWKV_GROUPS)], axis=-1))

    y = jnp.concatenate([yc[:, i * D:(i + 1) * D] for i in range(RWKV_PAIR) for yc in ys], axis=0)
    inv_n = 1.0 / HEAD_DIM
    mean = jnp.dot(y.astype(BF16), hb, preferred_element_type=F32) * inv_n
    yc = y - mean
    var = jnp.dot((yc * yc).astype(BF16), hb, preferred_element_type=F32) * inv_n
    y = yc * lax.rsqrt(var + GN_EPS) * gnw_ref[...] + gnb_ref[...]
    rk = jnp.dot((r * k * rk_ref[...]).astype(BF16), hb, preferred_element_type=F32)
    o_ref[...] = ((y + rk * v) * g).reshape(RWKV_PAIR, tt, D).astype(o_ref.dtype)


def _rwkv_call(rw, mu, w0, w2, a0, a2, g2, k_k, k_a, r_k, gn_w, gn_b, *, n_chunks):
    bsz, t, cols = rw.shape
    C, D = RWKV_CHUNK, RWKV_DIM
    tt = n_chunks * C
    lora = jnp.zeros((LORA_COLS, 3 * D), F32)
    lora = lora.at[0:DECAY_LORA, 0:D].set(w2)
    lora = lora.at[DECAY_LORA:DECAY_LORA + ICLR_LORA, D:2 * D].set(a2)
    lora = lora.at[DECAY_LORA + ICLR_LORA:, 2 * D:].set(g2)
    head_id = jnp.arange(D) // HEAD_DIM
    hb = (head_id[:, None] == head_id[None, :]).astype(BF16)
    pos = jnp.arange(RWKV_PAIR * tt)
    tril = ((pos[:, None] >= pos[None, :]) & (pos[:, None] // C == pos[None, :] // C)).astype(BF16)
    vec = lambda x: x.reshape(1, -1).astype(F32)
    vecs = [vec(mu), vec(w0), vec(a0), vec(k_k), vec(k_a), vec(r_k), vec(gn_w), vec(gn_b)]
    consts = vecs + [lora.astype(BF16), hb, tril]
    kern = functools.partial(_rwkv_kernel, n_chunks=n_chunks)
    return pl.pallas_call(
        kern,
        grid=(bsz // RWKV_PAIR, t // tt),
        in_specs=[pl.BlockSpec((RWKV_PAIR, tt, cols), lambda b, c: (b, c, 0))] + [_const_spec(x.shape) for x in consts],
        out_specs=pl.BlockSpec((RWKV_PAIR, tt, D), lambda b, c: (b, c, 0)),
        out_shape=jax.ShapeDtypeStruct((bsz, t, D), BF16),
        scratch_shapes=[pltpu.VMEM((RWKV_PAIR, 1, cols), F32),
                        pltpu.VMEM((RWKV_GROUPS, RWKV_GROUP_W, RWKV_GROUP_W), F32)],
        compiler_params=_params("parallel", "arbitrary"),
    )(rw, *consts)


SSM_TILE_GROUPS = 2
SSM_TILES = SSM_GROUPS // SSM_TILE_GROUPS
SSM_HALF = SSM_TILE_GROUPS * SSM_STATE
SSM_LANE_TILE = SSM_DIM // 2


def _s5_operators(lam_re, lam_im, log_dt, b_re, b_im, c_re, c_im, n_levels):
    L, G, P, H = SSM_CHUNK, SSM_GROUPS, SSM_STATE, SSM_GROUP_CH
    hi = lax.Precision.HIGHEST
    lr = jnp.minimum(lam_re.astype(F32), -1e-4)
    li = lam_im.astype(F32)
    dt = jnp.exp(log_dt.astype(F32))[:, None]
    n = jnp.arange(L + 1, dtype=F32)[:, None, None]
    mag = jnp.exp(lr * dt * n)
    pr = mag * jnp.cos(li * dt * n)
    pi = mag * jnp.sin(li * dt * n)
    den = lr * lr + li * li
    nr, ni = pr[1] - 1.0, pi[1]
    fr = (nr * lr + ni * li) / den
    fi = (ni * lr - nr * li) / den
    b_re, b_im = b_re.astype(F32), b_im.astype(F32)
    bb_re = fr[..., None] * b_re - fi[..., None] * b_im
    bb_im = fr[..., None] * b_im + fi[..., None] * b_re
    c_re, c_im = c_re.astype(F32), c_im.astype(F32)
    cl_re = c_re[None] * pr[:L, :, None, :] - c_im[None] * pi[:L, :, None, :]
    cl_im = c_re[None] * pi[:L, :, None, :] + c_im[None] * pr[:L, :, None, :]
    kern = (jnp.einsum('nghp,gpk->nghk', cl_re, bb_re, precision=hi)
            - jnp.einsum('nghp,gpk->nghk', cl_im, bb_im, precision=hi))
    eye = jnp.eye(G, dtype=F32)
    k_lag = jnp.einsum('lgab,gk->lgbka', kern, eye).reshape(L, G * H, G * H)
    b_full = lambda bb: jnp.einsum('gph,gk->ghkp', bb, eye).reshape(G * H, G * P)
    c_full = lambda cc: jnp.einsum('ghp,gk->gpkh', cc, eye).reshape(G * P, G * H)
    tile_cols = lambda m: m.reshape(G * H, SSM_TILES, SSM_HALF).transpose(1, 0, 2)
    b_cat = jnp.concatenate([tile_cols(b_full(bb_re)), tile_cols(b_full(bb_im))], axis=-1)
    tile_rows = lambda m: m.reshape(SSM_TILES, SSM_HALF, G * H)
    c_cat = jnp.concatenate([tile_rows(c_full(c_re)), -tile_rows(c_full(c_im))], axis=1)
    lam = lambda x: x.reshape(SSM_TILES, 1, SSM_HALF)
    qr, qi = pr[L], pi[L]
    scan_r, scan_i = [], []
    for _ in range(n_levels):
        scan_r.append(lam(qr))
        scan_i.append(lam(qi))
        qr, qi = qr * qr - qi * qi, 2.0 * qr * qi
    return (k_lag.astype(BF16), b_cat.astype(BF16), c_cat.astype(BF16), lam(pr[1]), lam(pi[1]),
            jnp.stack(scan_r), jnp.stack(scan_i))


def _cmul(x, cr, ci):
    re, im = x[:, :SSM_HALF], x[:, SSM_HALF:]
    return jnp.concatenate([cr * re - ci * im, cr * im + ci * re], axis=1)


def _s5_kernel(ulo_ref, uhi_ref, klag_ref, bcat_ref, ccat_ref, stepr_ref, stepi_ref, scanr_ref, scani_ref, d_ref,
               gluw_ref, glub_ref, olo_ref, ohi_ref, ut_ref, utb_ref, acc_ref, p_ref, z_ref, *, n_chunks, n_levels):
    L, nc = SSM_CHUNK, n_chunks
    for s in range(L):
        rows = jnp.concatenate([ref[pl.ds(s, nc, stride=L), :] for ref in (ulo_ref, uhi_ref)], axis=1)
        ut_ref[s * nc:(s + 1) * nc, :] = rows
        utb_ref[s * nc:(s + 1) * nc, :] = rows.astype(BF16)
    acc_ref[...] = d_ref[...] * ut_ref[...]
    for lag in range(L):
        acc_ref[lag * nc:, :] += jnp.dot(utb_ref[0:(L - lag) * nc, :], klag_ref[lag], preferred_element_type=F32)
    row = lax.broadcasted_iota(jnp.int32, (nc, 2 * SSM_HALF), 0)

    def tile(j, carry):
        cr, ci = stepr_ref[j], stepi_ref[j]
        p_ref[...] = jnp.dot(utb_ref[...], bcat_ref[j], preferred_element_type=F32)
        x = p_ref[0:nc, :]
        for s in range(1, L):
            x = _cmul(x, cr, ci) + p_ref[s * nc:(s + 1) * nc, :]
        for k in range(n_levels):
            sh = jnp.where(row >= (1 << k), pltpu.roll(x, 1 << k, 0), 0.0)
            x = x + _cmul(sh, scanr_ref[k, j], scani_ref[k, j])
        z = jnp.where(row >= 1, pltpu.roll(x, 1, 0), 0.0)
        for t in range(L):
            z = _cmul(z, cr, ci)
            z_ref[t * nc:(t + 1) * nc, :] = z.astype(BF16)
        acc_ref[...] += jnp.dot(z_ref[...], ccat_ref[j], preferred_element_type=F32)
        return carry

    lax.fori_loop(0, SSM_TILES, tile, 0)
    y = jax.nn.gelu(acc_ref[...])
    gate = jnp.dot(y.astype(BF16), gluw_ref[...], preferred_element_type=F32) + glub_ref[...]
    acc_ref[...] = y * jax.nn.sigmoid(gate)
    for t in range(L):
        olo_ref[pl.ds(t, nc, stride=L), :] = acc_ref[t * nc:(t + 1) * nc, :SSM_LANE_TILE]
        ohi_ref[pl.ds(t, nc, stride=L), :] = acc_ref[t * nc:(t + 1) * nc, SSM_LANE_TILE:]


def _s5_call(su_lo, su_hi, ops, d, glu_w, glu_b):
    bsz, t, half = su_lo.shape
    width = 2 * half
    nc = t // SSM_CHUNK
    n_levels = ops[-1].shape[0]
    consts = list(ops) + [d.reshape(1, width).astype(F32), glu_w, glu_b.reshape(1, width).astype(F32)]
    kern = functools.partial(_s5_kernel, n_chunks=nc, n_levels=n_levels)
    seq = pl.BlockSpec((None, t, half), lambda b: (b, 0, 0))
    return pl.pallas_call(
        kern,
        grid=(bsz,),
        in_specs=[seq, seq] + [_const_spec(x.shape) for x in consts],
        out_specs=[seq, seq],
        out_shape=[jax.ShapeDtypeStruct((bsz, t, half), F32)] * 2,
        scratch_shapes=[pltpu.VMEM((t, width), F32), pltpu.VMEM((t, width), BF16), pltpu.VMEM((t, width), F32),
                        pltpu.VMEM((t, 2 * SSM_HALF), F32), pltpu.VMEM((t, 2 * SSM_HALF), BF16)],
        compiler_params=_params("parallel"),
    )(su_lo, su_hi, *consts)


def _outproj_kernel(h_ref, ya_ref, yr_ref, yslo_ref, yshi_ref, wout_ref, g_ref, o_ref):
    y = jnp.concatenate([ya_ref[...], yr_ref[...], yslo_ref[...].astype(BF16), yshi_ref[...].astype(BF16)], axis=-1)
    m = jnp.dot(y, wout_ref[...], preferred_element_type=F32)
    o_ref[...] = h_ref[...] + _rms(m, g_ref[...])


def _outproj_call(h, ys, w_out, g_post, *, tm):
    n, d = h.shape
    row = lambda w: pl.BlockSpec((tm, w), lambda i: (i, 0))
    return pl.pallas_call(
        _outproj_kernel,
        grid=(n // tm,),
        in_specs=[row(d)] + [row(y.shape[-1]) for y in ys] + [_const_spec(w_out.shape), _const_spec((1, d))],
        out_specs=row(d),
        out_shape=jax.ShapeDtypeStruct((n, d), F32),
        compiler_params=_params("parallel"),
    )(h, *ys, w_out, g_post.reshape(1, d))


def kernel(x, rel_bias, ln_pre_ffn1, ln_post_ffn1, ffn1_w_gate, ffn1_w_up, ffn1_w_down, ln_pre_mix, ln_post_mix, w_in, w_out, att_sinks, rwkv_mu, rwkv_w0, rwkv_w2, rwkv_a0, rwkv_a2, rwkv_g2, rwkv_k_k, rwkv_k_a, rwkv_r_k, rwkv_gn_w, rwkv_gn_b, ssm_lambda_re, ssm_lambda_im, ssm_log_dt, ssm_b_re, ssm_b_im, ssm_c_re, ssm_c_im, ssm_d, ssm_glu_w, ssm_glu_b, ln_pre_ffn2, ln_post_ffn2, ffn2_w_gate, ffn2_w_up, ffn2_w_down):
    bsz, t, d = x.shape
    n = bsz * t
    depth = w_in.shape[0]
    tm = 512 if n % 512 == 0 else 128
    ff_chunk = 256
    rwkv_chunks = 2 if t % (2 * RWKV_CHUNK) == 0 else 1
    bf = lambda w: w.astype(BF16)
    bias = _band_bias(rel_bias)
    h = x.reshape(n, d)
    for l in range(depth):
        h = _ffn_call(h, ln_pre_ffn1[l], ln_post_ffn1[l], bf(ffn1_w_gate[l]), bf(ffn1_w_up[l]), bf(ffn1_w_down[l]),
                      tm=tm, ff_chunk=ff_chunk)
        q, k, v, rw, su_lo, su_hi = _inproj_call(h, ln_pre_mix[l], bf(w_in[l]), tm=tm)
        seq = lambda z: z.reshape(bsz, t, z.shape[-1])
        ya = _attn_call(seq(q), seq(k), seq(v), bias, att_sinks[l].astype(F32))
        yr = _rwkv_call(seq(rw), rwkv_mu[l], rwkv_w0[l], rwkv_w2[l], rwkv_a0[l], rwkv_a2[l], rwkv_g2[l],
                        rwkv_k_k[l], rwkv_k_a[l], rwkv_r_k[l], rwkv_gn_w[l], rwkv_gn_b[l], n_chunks=rwkv_chunks)
        ops = _s5_operators(ssm_lambda_re[l], ssm_lambda_im[l], ssm_log_dt[l], ssm_b_re[l], ssm_b_im[l],
                            ssm_c_re[l], ssm_c_im[l], n_levels=max(1, math.ceil(math.log2(t // SSM_CHUNK))))
        ys_lo, ys_hi = _s5_call(seq(su_lo), seq(su_hi), ops, ssm_d[l], bf(ssm_glu_w[l]), ssm_glu_b[l])
        flat = lambda z: z.reshape(n, z.shape[-1])
        h = _outproj_call(h, [flat(ya), flat(yr), flat(ys_lo), flat(ys_hi)], bf(w_out[l]), ln_post_mix[l], tm=tm)
        h = _ffn_call(h, ln_pre_ffn2[l], ln_post_ffn2[l], bf(ffn2_w_gate[l]), bf(ffn2_w_up[l]), bf(ffn2_w_down[l]),
                      tm=tm, ff_chunk=ff_chunk)
    return h.reshape(bsz, t, d)
```

```python
import functools
import math

import jax
import jax.numpy as jnp
from jax import lax
from jax.experimental import pallas as pl
from jax.experimental.pallas import tpu as pltpu

F32 = jnp.float32
BF16 = jnp.bfloat16

HEAD_DIM = 64
ATT_HEADS = 6
ATT_KV_HEADS = 2
ATT_GROUP = ATT_HEADS // ATT_KV_HEADS
ATT_DIM = ATT_HEADS * HEAD_DIM
ATT_KV_DIM = ATT_KV_HEADS * HEAD_DIM
WINDOW = 128
N_BUCKETS = 32
MAX_DISTANCE = 128
RWKV_HEADS = 6
RWKV_DIM = RWKV_HEADS * HEAD_DIM
DECAY_LORA = 32
ICLR_LORA = 32
GATE_LORA = 64
LORA_COLS = DECAY_LORA + ICLR_LORA + GATE_LORA
RWKV_COLS = 3 * RWKV_DIM + LORA_COLS
GN_EPS = 64e-5
SSM_GROUPS = 16
SSM_GROUP_CH = 16
SSM_DIM = SSM_GROUPS * SSM_GROUP_CH
SSM_STATE = 64
RMS_EPS = 1e-6
MASK_VALUE = -1e30

VMEM_LIMIT_BYTES = 56 * 1024 * 1024
RWKV_CHUNK = 64
SSM_CHUNK = 16


def _params(*sem):
    return pltpu.CompilerParams(dimension_semantics=sem, vmem_limit_bytes=VMEM_LIMIT_BYTES)


def _rms(x, g):
    return x * lax.rsqrt(jnp.mean(x * x, axis=-1, keepdims=True) + RMS_EPS) * g


def _const_spec(shape):
    nd = len(shape)
    return pl.BlockSpec(shape, lambda *_: (0,) * nd, pipeline_mode=pl.Buffered(1))


_IN_SPLITS = ((ATT_DIM, BF16), (ATT_KV_DIM, BF16), (ATT_KV_DIM, BF16), (RWKV_COLS, F32),
              (SSM_DIM // 2, F32), (SSM_DIM // 2, F32))
_N_MIX = 4


def _ffn_kernel(*refs, ff_chunk, n_chunks, mix_in, proj_out):
    it = iter(refs)
    x_ref = next(it)
    if mix_in:
        y_refs = [next(it) for _ in range(_N_MIX)]
        wout_ref, gmix_ref = next(it), next(it)
    gpre_ref, gpost_ref, wg_ref, wu_ref, wd_ref = (next(it) for _ in range(5))
    if proj_out:
        gin_ref, win_ref = next(it), next(it)
    o_ref = next(it)
    if proj_out:
        proj_refs = [next(it) for _ in _IN_SPLITS]
    acc_ref = next(it)

    x = x_ref[...]
    if mix_in:
        y = jnp.concatenate([r[...].astype(BF16) for r in y_refs], axis=-1)
        x = x + _rms(jnp.dot(y, wout_ref[...], preferred_element_type=F32), gmix_ref[...])
    xn = _rms(x, gpre_ref[...]).astype(BF16)
    for c in range(n_chunks):
        sl = slice(c * ff_chunk, (c + 1) * ff_chunk)
        g = jnp.dot(xn, wg_ref[:, sl], preferred_element_type=F32)
        u = jnp.dot(xn, wu_ref[:, sl], preferred_element_type=F32)
        a = (g * jax.nn.sigmoid(g) * u).astype(BF16)
        d = jnp.dot(a, wd_ref[sl, :], preferred_element_type=F32)
        if c == 0:
            acc_ref[...] = d
        else:
            acc_ref[...] += d
    out = x + 0.5 * _rms(acc_ref[...], gpost_ref[...])
    o_ref[...] = out
    if proj_out:
        p = jnp.dot(_rms(out, gin_ref[...]).astype(BF16), win_ref[...], preferred_element_type=F32)
        off = 0
        for ref, (width, dtype) in zip(proj_refs, _IN_SPLITS):
            ref[...] = p[:, off:off + width].astype(dtype)
            off += width


def _ffn_call(h, g_pre, g_post, wg, wu, wd, *, tm, ff_chunk, mix=None, proj=None):
    n, d = h.shape
    dff = wg.shape[1]
    row = lambda w: pl.BlockSpec((tm, w), lambda i: (i, 0))
    vec = lambda g: g.reshape(1, d)
    args, specs = [h], [row(d)]
    if mix is not None:
        ys, w_out, g_mix = mix
        args += list(ys) + [w_out, vec(g_mix)]
        specs += [row(y.shape[-1]) for y in ys] + [_const_spec(w_out.shape), _const_spec((1, d))]
    args += [vec(g_pre), vec(g_post), wg, wu, wd]
    specs += [_const_spec((1, d)), _const_spec((1, d)), _const_spec((d, dff)), _const_spec((d, dff)),
              _const_spec((dff, d))]
    out_specs, out_shape = [row(d)], [jax.ShapeDtypeStruct((n, d), F32)]
    if proj is not None:
        g_in, w_in = proj
        args += [vec(g_in), w_in]
        specs += [_const_spec((1, d)), _const_spec(w_in.shape)]
        out_specs += [row(w) for w, _ in _IN_SPLITS]
        out_shape += [jax.ShapeDtypeStruct((n, w), dt) for w, dt in _IN_SPLITS]
    kern = functools.partial(_ffn_kernel, ff_chunk=ff_chunk, n_chunks=dff // ff_chunk,
                             mix_in=mix is not None, proj_out=proj is not None)
    outs = pl.pallas_call(
        kern,
        grid=(n // tm,),
        in_specs=specs,
        out_specs=out_specs,
        out_shape=out_shape,
        scratch_shapes=[pltpu.VMEM((tm, d), F32)],
        compiler_params=_params("parallel"),
    )(*args)
    return outs if proj is not None else outs[0]


ATT_BLOCKS = 2


def _band_bias(rel_bias):
    qi = jnp.arange(WINDOW)[:, None]
    kj = jnp.arange(2 * WINDOW)[None, :]
    rel = qi + WINDOW - kj
    in_window = (rel >= 0) & (rel < WINDOW)
    n = jnp.maximum(rel, 0)
    max_exact = N_BUCKETS // 2
    nf = jnp.maximum(n, 1).astype(F32)
    large = max_exact + (jnp.log(nf / max_exact) / math.log(MAX_DISTANCE / max_exact)
                         * (N_BUCKETS - max_exact)).astype(jnp.int32)
    large = jnp.minimum(large, N_BUCKETS - 1)
    bucket = jnp.where(n < max_exact, n, large)
    onehot = (bucket[..., None] == jnp.arange(N_BUCKETS)).astype(F32)
    bias = jnp.einsum('qkb,bh->hqk', onehot, rel_bias.astype(F32), precision=lax.Precision.HIGHEST)
    return jnp.transpose(jnp.where(in_window[None], bias, MASK_VALUE), (0, 2, 1))


def _attn_kernel(sink_ref, q_ref, kp_ref, kc_ref, vp_ref, vc_ref, bias_ref, o_ref):
    first = pl.program_id(1) == 0
    q = q_ref[...] * (HEAD_DIM ** -0.5)
    k = jnp.concatenate([kp_ref[...], kc_ref[...]], axis=0)
    v = jnp.concatenate([vp_ref[...], vc_ref[...]], axis=0)
    row = lax.broadcasted_iota(jnp.int32, (1, 2 * WINDOW, 1), 1)
    has_prev = jnp.logical_or(jnp.logical_not(first), row >= WINDOW)
    sink = sink_ref[...]
    kv_of = [h // ATT_GROUP for h in range(ATT_HEADS)]
    head = lambda x, j: x[:, j * HEAD_DIM:(j + 1) * HEAD_DIM]
    blocks = []
    for i in range(ATT_BLOCKS):
        band = slice(i * WINDOW, (i + 2) * WINDOW)
        qi = q[i * WINDOW:(i + 1) * WINDOW]
        v_t = v[band].astype(F32).T.astype(BF16)
        qs = jnp.stack([head(qi, h) for h in range(ATT_HEADS)], axis=0)
        ks = jnp.stack([head(k[band], j) for j in kv_of], axis=0)
        vs = jnp.stack([v_t[j * HEAD_DIM:(j + 1) * HEAD_DIM] for j in kv_of], axis=0)
        s = lax.dot_general(ks, qs, (((2,), (2,)), ((0,), (0,))), preferred_element_type=F32)
        s = s + bias_ref[...]
        if i == 0:
            s = jnp.where(has_prev, s, MASK_VALUE)
        m = jnp.maximum(jnp.max(s, axis=1, keepdims=True), sink)
        p = jnp.exp(s - m)
        denom = jnp.sum(p, axis=1, keepdims=True) + jnp.exp(sink - m)
        o = lax.dot_general(vs, p.astype(BF16), (((2,), (1,)), ((0,), (0,))), preferred_element_type=F32)
        o = o / denom
        blocks.append(o.reshape(ATT_DIM, WINDOW).T)
    o_ref[...] = jnp.concatenate(blocks, axis=0).astype(o_ref.dtype)


def _attn_call(q, k, v, bias, sinks):
    bsz, t, _ = q.shape
    span = ATT_BLOCKS * WINDOW
    cur = lambda b, n: (b, n, 0)
    prv = lambda b, n: (b, jnp.maximum(n * ATT_BLOCKS - 1, 0), 0)
    cur_kv = pl.BlockSpec((None, span, ATT_KV_DIM), cur)
    prv_kv = pl.BlockSpec((None, WINDOW, ATT_KV_DIM), prv)
    return pl.pallas_call(
        _attn_kernel,
        grid=(bsz, t // span),
        in_specs=[
            _const_spec((ATT_HEADS, 1, 1)),
            pl.BlockSpec((None, span, ATT_DIM), cur),
            prv_kv, cur_kv, prv_kv, cur_kv,
            _const_spec((ATT_HEADS, 2 * WINDOW, WINDOW)),
        ],
        out_specs=pl.BlockSpec((None, span, ATT_DIM), cur),
        out_shape=jax.ShapeDtypeStruct((bsz, t, ATT_DIM), BF16),
        compiler_params=_params("parallel", "parallel"),
    )(sinks.reshape(ATT_HEADS, 1, 1), q, k, k, v, v, bias)


RWKV_PAIR = 2
RWKV_GROUP_HEADS = 4
RWKV_GROUP_W = RWKV_GROUP_HEADS * HEAD_DIM
RWKV_GROUPS = RWKV_PAIR * RWKV_HEADS // RWKV_GROUP_HEADS


def _gdot(a, b, contract):
    return lax.dot_general(a.astype(BF16), b.astype(BF16), (contract, ((0,), (0,))),
                           preferred_element_type=F32)


def _gmm(a, b):
    return _gdot(a, b, ((2,), (1,)))


def _gmm_nt(a, b):
    return _gdot(a, b, ((2,), (2,)))


def _gmm_tn(a, b):
    return _gdot(a, b, ((1,), (1,)))


def _flat_groups(x):
    return jnp.stack([x[:, j * RWKV_GROUP_W:(j + 1) * RWKV_GROUP_W] for j in range(RWKV_GROUPS)], axis=0)


def _head_rows(x):
    lane_head = lax.broadcasted_iota(jnp.int32, (1, RWKV_GROUP_W), 1) // HEAD_DIM
    zero = jnp.zeros((), x.dtype)
    groups = []
    for j in range(RWKV_GROUPS):
        xg = x[:, j * RWKV_GROUP_W:(j + 1) * RWKV_GROUP_W]
        groups.append(jnp.concatenate([jnp.where(lane_head == i, xg, zero) for i in range(RWKV_GROUP_HEADS)], axis=0))
    return jnp.stack(groups, axis=0)


def _rwkv_kernel(p_ref, mu_ref, w0_ref, a0_ref, kk_ref, ka_ref, rk_ref, gnw_ref, gnb_ref,
                 lora_ref, hb_ref, tril_ref, o_ref, prev_ref, s_ref, *, n_chunks):
    C, D, W = RWKV_CHUNK, RWKV_DIM, RWKV_GROUP_W
    tt = n_chunks * C

    @pl.when(pl.program_id(1) == 0)
    def _():
        prev_ref[...] = jnp.zeros_like(prev_ref)
        s_ref[...] = jnp.zeros_like(s_ref)

    p = p_ref[...].reshape(RWKV_PAIR * tt, RWKV_COLS)
    row = lax.broadcasted_iota(jnp.int32, p.shape, 0)
    prev = pltpu.roll(p, 1, 0)
    for i in range(RWKV_PAIR):
        prev = jnp.where(row == i * tt, prev_ref[i], prev)
        prev_ref[i] = p[(i + 1) * tt - 1:(i + 1) * tt, :]
    xs = p + (prev - p) * mu_ref[...]
    r, k, v = xs[:, 0:D], xs[:, D:2 * D], xs[:, 2 * D:3 * D]
    lo = xs[:, 3 * D:3 * D + LORA_COLS]
    lane = lax.broadcasted_iota(jnp.int32, lo.shape, 1)
    act = jnp.where(lane < DECAY_LORA, jnp.tanh(lo),
                    jnp.where(lane < DECAY_LORA + ICLR_LORA, lo, jax.nn.sigmoid(lo)))
    proj = jnp.dot(act.astype(BF16), lora_ref[...], preferred_element_type=F32)
    z = -(w0_ref[...] + proj[:, 0:D])
    softplus = jnp.maximum(z, 0.0) + jnp.log(1.0 + jnp.exp(-jnp.abs(z)))
    logw = -jnp.exp(-softplus - 0.5)
    a = jax.nn.sigmoid(a0_ref[...] + proj[:, D:2 * D])
    g = proj[:, 2 * D:3 * D]
    hb = hb_ref[...]
    kk = k * kk_ref[...]
    ss = jnp.dot((kk * kk).astype(BF16), hb, preferred_element_type=F32)
    kk = kk / jnp.maximum(jnp.sqrt(ss), 1e-12)
    k = k * (1.0 + (a - 1.0) * ka_ref[...])

    logw_hi = logw.astype(BF16)
    logw_lo = (logw - logw_hi.astype(F32)).astype(BF16)
    cum = (jnp.dot(tril_ref[...], logw_hi, preferred_element_type=F32)
           + jnp.dot(tril_ref[...], logw_lo, preferred_element_type=F32))
    e_pos = jnp.exp(cum)
    e_neg = jnp.exp(-cum)
    kb = kk * a
    r_t = (r * e_pos).astype(BF16)
    k_t = (k * e_neg).astype(BF16)
    b_t = (kb * e_neg).astype(BF16)
    a_t = (-kk * jnp.exp(cum - logw)).astype(BF16)
    v_b = v.astype(BF16)

    r64 = lax.broadcasted_iota(jnp.int32, (RWKV_GROUP_HEADS * C, W), 0) % C
    c64 = lax.broadcasted_iota(jnp.int32, (RWKV_GROUP_HEADS * C, W), 1) % C
    strict = (r64 > c64)[None]
    eye = (lax.broadcasted_iota(jnp.int32, (RWKV_GROUP_HEADS * C, W), 0)
           == lax.broadcasted_iota(jnp.int32, (RWKV_GROUP_HEADS * C, W), 1))[None].astype(F32)
    incl = (lax.broadcasted_iota(jnp.int32, (C, W), 0) >= lax.broadcasted_iota(jnp.int32, (C, W), 1) % C)[None]

    def pair_lanes(x, c):
        return jnp.concatenate([x[i * tt + c * C:i * tt + (c + 1) * C] for i in range(RWKV_PAIR)], axis=1)

    ys = []
    for c in range(n_chunks):
        g_end = pair_lanes(e_pos, c)[C - 1:C, :]
        k_end = (pair_lanes(k * e_neg, c) * g_end).astype(BF16)
        b_end = (pair_lanes(kb * e_neg, c) * g_end).astype(BF16)
        ax, bx, kx = _head_rows(pair_lanes(a_t, c)), _head_rows(pair_lanes(b_t, c)), _head_rows(pair_lanes(k_t, c))
        vx = _head_rows(pair_lanes(v_b, c))
        rf = _flat_groups(pair_lanes(r_t, c))
        sc = _gmm_nt(jnp.concatenate([ax, rf], axis=1), jnp.concatenate([bx, kx], axis=1))
        a_ab = jnp.where(strict, sc[:, :4 * C, :W], 0.0)
        a_ak = jnp.where(strict, sc[:, :4 * C, W:], 0.0)
        a_rb = jnp.where(incl, sc[:, 4 * C:, :W], 0.0)
        a_rk = jnp.where(incl, sc[:, 4 * C:, W:], 0.0)
        inv = eye + a_ab
        powr = _gmm(a_ab, a_ab)
        span = 2
        while span < C:
            if 2 * span < C:
                both = _gmm(powr, jnp.concatenate([inv.astype(BF16), powr.astype(BF16)], axis=-1))
                inv, powr = inv + both[..., :W], both[..., W:]
            else:
                inv = inv + _gmm(powr, inv)
            span *= 2
        akv = _gmm(a_ak, vx)

        s = s_ref[...]
        sr = _gmm_nt(jnp.concatenate([ax, rf], axis=1), s)
        u = _gmm(inv, sr[:, :4 * C] + akv)
        uv = jnp.concatenate([u.astype(BF16), vx], axis=1)
        y = sr[:, 4 * C:] + _gmm(jnp.concatenate([a_rb, a_rk], axis=-1), uv)
        s_ref[...] = (s * _flat_groups(g_end)
                      + _gmm_tn(uv, jnp.concatenate([_head_rows(b_end), _head_rows(k_end)], axis=1)))
        ys.append(jnp.concatenate([y[j] for j in range(RWKV_GROUPS)], axis=-1))

    y = jnp.concatenate([yc[:, i * D:(i + 1) * D] for i in range(RWKV_PAIR) for yc in ys], axis=0)
    inv_n = 1.0 / HEAD_DIM
    mean = jnp.dot(y.astype(BF16), hb, preferred_element_type=F32) * inv_n
    yc = y - mean
    var = jnp.dot((yc * yc).astype(BF16), hb, preferred_element_type=F32) * inv_n
    y = yc * lax.rsqrt(var + GN_EPS) * gnw_ref[...] + gnb_ref[...]
    rk = jnp.dot((r * k * rk_ref[...]).astype(BF16), hb, preferred_element_type=F32)
    o_ref[...] = ((y + rk * v) * g).reshape(RWKV_PAIR, tt, D).astype(o_ref.dtype)


def _rwkv_call(rw, mu, w0, w2, a0, a2, g2, k_k, k_a, r_k, gn_w, gn_b, *, n_chunks):
    bsz, t, cols = rw.shape
    C, D = RWKV_CHUNK, RWKV_DIM
    tt = n_chunks * C
    lora = jnp.zeros((LORA_COLS, 3 * D), F32)
    lora = lora.at[0:DECAY_LORA, 0:D].set(w2)
    lora = lora.at[DECAY_LORA:DECAY_LORA + ICLR_LORA, D:2 * D].set(a2)
    lora = lora.at[DECAY_LORA + ICLR_LORA:, 2 * D:].set(g2)
    head_id = jnp.arange(D) // HEAD_DIM
    hb = (head_id[:, None] == head_id[None, :]).astype(BF16)
    pos = jnp.arange(RWKV_PAIR * tt)
    tril = ((pos[:, None] >= pos[None, :]) & (pos[:, None] // C == pos[None, :] // C)).astype(BF16)
    vec = lambda x: x.reshape(1, -1).astype(F32)
    vecs = [vec(mu), vec(w0), vec(a0), vec(k_k), vec(k_a), vec(r_k), vec(gn_w), vec(gn_b)]
    consts = vecs + [lora.astype(BF16), hb, tril]
    kern = functools.partial(_rwkv_kernel, n_chunks=n_chunks)
    return pl.pallas_call(
        kern,
        grid=(bsz // RWKV_PAIR, t // tt),
        in_specs=[pl.BlockSpec((RWKV_PAIR, tt, cols), lambda b, c: (b, c, 0))] + [_const_spec(x.shape) for x in consts],
        out_specs=pl.BlockSpec((RWKV_PAIR, tt, D), lambda b, c: (b, c, 0)),
        out_shape=jax.ShapeDtypeStruct((bsz, t, D), BF16),
        scratch_shapes=[pltpu.VMEM((RWKV_PAIR, 1, cols), F32),
                        pltpu.VMEM((RWKV_GROUPS, RWKV_GROUP_W, RWKV_GROUP_W), F32)],
        compiler_params=_params("parallel", "arbitrary"),
    )(rw, *consts)


SSM_TILE_GROUPS = 2
SSM_TILES = SSM_GROUPS // SSM_TILE_GROUPS
SSM_HALF = SSM_TILE_GROUPS * SSM_STATE
SSM_LANE_TILE = SSM_DIM // 2


def _s5_operators(lam_re, lam_im, log_dt, b_re, b_im, c_re, c_im, n_levels):
    L, G, P, H = SSM_CHUNK, SSM_GROUPS, SSM_STATE, SSM_GROUP_CH
    hi = lax.Precision.HIGHEST
    lr = jnp.minimum(lam_re.astype(F32), -1e-4)
    li = lam_im.astype(F32)
    dt = jnp.exp(log_dt.astype(F32))[:, None]
    n = jnp.arange(L + 1, dtype=F32)[:, None, None]
    mag = jnp.exp(lr * dt * n)
    pr = mag * jnp.cos(li * dt * n)
    pi = mag * jnp.sin(li * dt * n)
    den = lr * lr + li * li
    nr, ni = pr[1] - 1.0, pi[1]
    fr = (nr * lr + ni * li) / den
    fi = (ni * lr - nr * li) / den
    b_re, b_im = b_re.astype(F32), b_im.astype(F32)
    bb_re = fr[..., None] * b_re - fi[..., None] * b_im
    bb_im = fr[..., None] * b_im + fi[..., None] * b_re
    c_re, c_im = c_re.astype(F32), c_im.astype(F32)
    cl_re = c_re[None] * pr[:L, :, None, :] - c_im[None] * pi[:L, :, None, :]
    cl_im = c_re[None] * pi[:L, :, None, :] + c_im[None] * pr[:L, :, None, :]
    kern = (jnp.einsum('nghp,gpk->nghk', cl_re, bb_re, precision=hi)
            - jnp.einsum('nghp,gpk->nghk', cl_im, bb_im, precision=hi))
    eye = jnp.eye(G, dtype=F32)
    k_lag = jnp.einsum('lgab,gk->lgbka', kern, eye).reshape(L, G * H, G * H)
    b_full = lambda bb: jnp.einsum('gph,gk->ghkp', bb, eye).reshape(G * H, G * P)
    c_full = lambda cc: jnp.einsum('ghp,gk->gpkh', cc, eye).reshape(G * P, G * H)
    tile_cols = lambda m: m.reshape(G * H, SSM_TILES, SSM_HALF).transpose(1, 0, 2)
    b_cat = jnp.concatenate([tile_cols(b_full(bb_re)), tile_cols(b_full(bb_im))], axis=-1)
    tile_rows = lambda m: m.reshape(SSM_TILES, SSM_HALF, G * H)
    c_cat = jnp.concatenate([tile_rows(c_full(c_re)), -tile_rows(c_full(c_im))], axis=1)
    lam = lambda x: x.reshape(SSM_TILES, 1, SSM_HALF)
    qr, qi = pr[L], pi[L]
    scan_r, scan_i = [], []
    for _ in range(n_levels):
        scan_r.append(lam(qr))
        scan_i.append(lam(qi))
        qr, qi = qr * qr - qi * qi, 2.0 * qr * qi
    return (k_lag.astype(BF16), b_cat.astype(BF16), c_cat.astype(BF16), lam(pr[1]), lam(pi[1]),
            jnp.stack(scan_r), jnp.stack(scan_i))


def _cmul(x, cr, ci):
    re, im = x[:, :SSM_HALF], x[:, SSM_HALF:]
    return jnp.concatenate([cr * re - ci * im, cr * im + ci * re], axis=1)


def _s5_kernel(ulo_ref, uhi_ref, klag_ref, bcat_ref, ccat_ref, stepr_ref, stepi_ref, scanr_ref, scani_ref, d_ref,
               gluw_ref, glub_ref, olo_ref, ohi_ref, ut_ref, utb_ref, acc_ref, p_ref, z_ref, *, n_chunks, n_levels):
    L, nc = SSM_CHUNK, n_chunks
    for s in range(L):
        rows = jnp.concatenate([ref[pl.ds(s, nc, stride=L), :] for ref in (ulo_ref, uhi_ref)], axis=1)
        ut_ref[s * nc:(s + 1) * nc, :] = rows
        utb_ref[s * nc:(s + 1) * nc, :] = rows.astype(BF16)
    acc_ref[...] = d_ref[...] * ut_ref[...]
    for lag in range(L):
        acc_ref[lag * nc:, :] += jnp.dot(utb_ref[0:(L - lag) * nc, :], klag_ref[lag], preferred_element_type=F32)
    row = lax.broadcasted_iota(jnp.int32, (nc, 2 * SSM_HALF), 0)

    def tile(j, carry):
        cr, ci = stepr_ref[j], stepi_ref[j]
        p_ref[...] = jnp.dot(utb_ref[...], bcat_ref[j], preferred_element_type=F32)
        x = p_ref[0:nc, :]
        for s in range(1, L):
            x = _cmul(x, cr, ci) + p_ref[s * nc:(s + 1) * nc, :]
        for k in range(n_levels):
            sh = jnp.where(row >= (1 << k), pltpu.roll(x, 1 << k, 0), 0.0)
            x = x + _cmul(sh, scanr_ref[k, j], scani_ref[k, j])
        z = jnp.where(row >= 1, pltpu.roll(x, 1, 0), 0.0)
        for t in range(L):
            z = _cmul(z, cr, ci)
            z_ref[t * nc:(t + 1) * nc, :] = z.astype(BF16)
        acc_ref[...] += jnp.dot(z_ref[...], ccat_ref[j], preferred_element_type=F32)
        return carry

    lax.fori_loop(0, SSM_TILES, tile, 0)
    y = jax.nn.gelu(acc_ref[...])
    gate = jnp.dot(y.astype(BF16), gluw_ref[...], preferred_element_type=F32) + glub_ref[...]
    acc_ref[...] = y * jax.nn.sigmoid(gate)
    for t in range(L):
        olo_ref[pl.ds(t, nc, stride=L), :] = acc_ref[t * nc:(t + 1) * nc, :SSM_LANE_TILE]
        ohi_ref[pl.ds(t, nc, stride=L), :] = acc_ref[t * nc:(t + 1) * nc, SSM_LANE_TILE:]


def _s5_call(su_lo, su_hi, ops, d, glu_w, glu_b):
    bsz, t, half = su_lo.shape
    width = 2 * half
    nc = t // SSM_CHUNK
    n_levels = ops[-1].shape[0]
    consts = list(ops) + [d.reshape(1, width).astype(F32), glu_w, glu_b.reshape(1, width).astype(F32)]
    kern = functools.partial(_s5_kernel, n_chunks=nc, n_levels=n_levels)
    seq = pl.BlockSpec((None, t, half), lambda b: (b, 0, 0))
    return pl.pallas_call(
        kern,
        grid=(bsz,),
        in_specs=[seq, seq] + [_const_spec(x.shape) for x in consts],
        out_specs=[seq, seq],
        out_shape=[jax.ShapeDtypeStruct((bsz, t, half), F32)] * 2,
        scratch_shapes=[pltpu.VMEM((t, width), F32), pltpu.VMEM((t, width), BF16), pltpu.VMEM((t, width), F32),
                        pltpu.VMEM((t, 2 * SSM_HALF), F32), pltpu.VMEM((t, 2 * SSM_HALF), BF16)],
        compiler_params=_params("parallel"),
    )(su_lo, su_hi, *consts)


def kernel(x, rel_bias, ln_pre_ffn1, ln_post_ffn1, ffn1_w_gate, ffn1_w_up, ffn1_w_down, ln_pre_mix, ln_post_mix, w_in, w_out, att_sinks, rwkv_mu, rwkv_w0, rwkv_w2, rwkv_a0, rwkv_a2, rwkv_g2, rwkv_k_k, rwkv_k_a, rwkv_r_k, rwkv_gn_w, rwkv_gn_b, ssm_lambda_re, ssm_lambda_im, ssm_log_dt, ssm_b_re, ssm_b_im, ssm_c_re, ssm_c_im, ssm_d, ssm_glu_w, ssm_glu_b, ln_pre_ffn2, ln_post_ffn2, ffn2_w_gate, ffn2_w_up, ffn2_w_down):
    bsz, t, d = x.shape
    n = bsz * t
    depth = w_in.shape[0]
    tm = 512 if n % 512 == 0 else 128
    ff_chunk = 256
    rwkv_chunks = max(c for c in (4, 2, 1) if t % (c * RWKV_CHUNK) == 0)
    bf = lambda w: w.astype(BF16)
    bias = _band_bias(rel_bias)
    h = x.reshape(n, d)
    seq = lambda z: z.reshape(bsz, t, z.shape[-1])
    flat = lambda z: z.reshape(n, z.shape[-1])
    n_levels = max(1, math.ceil(math.log2(t // SSM_CHUNK)))
    for l in range(depth):
        h, q, k, v, rw, su_lo, su_hi = _ffn_call(
            h, ln_pre_ffn1[l], ln_post_ffn1[l], bf(ffn1_w_gate[l]), bf(ffn1_w_up[l]), bf(ffn1_w_down[l]),
            tm=tm, ff_chunk=ff_chunk, proj=(ln_pre_mix[l], bf(w_in[l])))
        ya = _attn_call(seq(q), seq(k), seq(v), bias, att_sinks[l].astype(F32))
        yr = _rwkv_call(seq(rw), rwkv_mu[l], rwkv_w0[l], rwkv_w2[l], rwkv_a0[l], rwkv_a2[l], rwkv_g2[l],
                        rwkv_k_k[l], rwkv_k_a[l], rwkv_r_k[l], rwkv_gn_w[l], rwkv_gn_b[l], n_chunks=rwkv_chunks)
        ops = _s5_operators(ssm_lambda_re[l], ssm_lambda_im[l], ssm_log_dt[l], ssm_b_re[l], ssm_b_im[l],
                            ssm_c_re[l], ssm_c_im[l], n_levels=n_levels)
        ys_lo, ys_hi = _s5_call(seq(su_lo), seq(su_hi), ops, ssm_d[l], bf(ssm_glu_w[l]), ssm_glu_b[l])
        h = _ffn_call(h, ln_pre_ffn2[l], ln_post_ffn2[l], bf(ffn2_w_gate[l]), bf(ffn2_w_up[l]), bf(ffn2_w_down[l]),
                      tm=tm, ff_chunk=ff_chunk,
                      mix=([flat(ya), flat(yr), flat(ys_lo), flat(ys_hi)], bf(w_out[l]), ln_post_mix[l]))
    return h.reshape(bsz, t, d)
```

```python
import functools
import math

import jax
import jax.numpy as jnp
from jax import lax
from jax.experimental import pallas as pl
from jax.experimental.pallas import tpu as pltpu

F32 = jnp.float32
BF16 = jnp.bfloat16

HEAD_DIM = 64
ATT_HEADS = 6
ATT_KV_HEADS = 2
ATT_GROUP = ATT_HEADS // ATT_KV_HEADS
ATT_DIM = ATT_HEADS * HEAD_DIM
ATT_KV_DIM = ATT_KV_HEADS * HEAD_DIM
WINDOW = 128
N_BUCKETS = 32
MAX_DISTANCE = 128
RWKV_HEADS = 6
RWKV_DIM = RWKV_HEADS * HEAD_DIM
DECAY_LORA = 32
ICLR_LORA = 32
GATE_LORA = 64
LORA_COLS = DECAY_LORA + ICLR_LORA + GATE_LORA
RWKV_COLS = 3 * RWKV_DIM + LORA_COLS
GN_EPS = 64e-5
SSM_GROUPS = 16
SSM_GROUP_CH = 16
SSM_DIM = SSM_GROUPS * SSM_GROUP_CH
SSM_STATE = 64
RMS_EPS = 1e-6
MASK_VALUE = -1e30

VMEM_LIMIT_BYTES = 56 * 1024 * 1024
RWKV_CHUNK = 64
SSM_CHUNK = 8


def _params(*sem):
    return pltpu.CompilerParams(dimension_semantics=sem, vmem_limit_bytes=VMEM_LIMIT_BYTES)


def _rms(x, g):
    return x * lax.rsqrt(jnp.mean(x * x, axis=-1, keepdims=True) + RMS_EPS) * g


def _const_spec(shape):
    nd = len(shape)
    return pl.BlockSpec(shape, lambda *_: (0,) * nd, pipeline_mode=pl.Buffered(1))


_IN_SPLITS = ((ATT_DIM, BF16), (ATT_KV_DIM, BF16), (ATT_KV_DIM, BF16), (RWKV_COLS, F32),
              (SSM_DIM // 2, F32), (SSM_DIM // 2, F32))
_N_MIX = 4


def _ffn_kernel(*refs, ff_chunk, n_chunks, mix_in, proj_out):
    it = iter(refs)
    x_ref = next(it)
    if mix_in:
        y_refs = [next(it) for _ in range(_N_MIX)]
        wout_ref, gmix_ref = next(it), next(it)
    gpre_ref, gpost_ref, wg_ref, wu_ref, wd_ref = (next(it) for _ in range(5))
    if proj_out:
        gin_ref, win_ref = next(it), next(it)
    o_ref = next(it)
    if proj_out:
        proj_refs = [next(it) for _ in _IN_SPLITS]
    acc_ref = next(it)

    x = x_ref[...]
    if mix_in:
        y = jnp.concatenate([r[...].astype(BF16) for r in y_refs], axis=-1)
        x = x + _rms(jnp.dot(y, wout_ref[...], preferred_element_type=F32), gmix_ref[...])
    xn = _rms(x, gpre_ref[...]).astype(BF16)
    for c in range(n_chunks):
        sl = slice(c * ff_chunk, (c + 1) * ff_chunk)
        g = jnp.dot(xn, wg_ref[:, sl], preferred_element_type=F32)
        u = jnp.dot(xn, wu_ref[:, sl], preferred_element_type=F32)
        a = (g * jax.nn.sigmoid(g) * u).astype(BF16)
        d = jnp.dot(a, wd_ref[sl, :], preferred_element_type=F32)
        if c == 0:
            acc_ref[...] = d
        else:
            acc_ref[...] += d
    out = x + 0.5 * _rms(acc_ref[...], gpost_ref[...])
    o_ref[...] = out
    if proj_out:
        p = jnp.dot(_rms(out, gin_ref[...]).astype(BF16), win_ref[...], preferred_element_type=F32)
        off = 0
        for ref, (width, dtype) in zip(proj_refs, _IN_SPLITS):
            ref[...] = p[:, off:off + width].astype(dtype)
            off += width


def _ffn_call(h, g_pre, g_post, wg, wu, wd, *, tm, ff_chunk, mix=None, proj=None):
    n, d = h.shape
    dff = wg.shape[1]
    row = lambda w: pl.BlockSpec((tm, w), lambda i: (i, 0))
    vec = lambda g: g.reshape(1, d)
    args, specs = [h], [row(d)]
    if mix is not None:
        ys, w_out, g_mix = mix
        args += list(ys) + [w_out, vec(g_mix)]
        specs += [row(y.shape[-1]) for y in ys] + [_const_spec(w_out.shape), _const_spec((1, d))]
    args += [vec(g_pre), vec(g_post), wg, wu, wd]
    specs += [_const_spec((1, d)), _const_spec((1, d)), _const_spec((d, dff)), _const_spec((d, dff)),
              _const_spec((dff, d))]
    out_specs, out_shape = [row(d)], [jax.ShapeDtypeStruct((n, d), F32)]
    if proj is not None:
        g_in, w_in = proj
        args += [vec(g_in), w_in]
        specs += [_const_spec((1, d)), _const_spec(w_in.shape)]
        out_specs += [row(w) for w, _ in _IN_SPLITS]
        out_shape += [jax.ShapeDtypeStruct((n, w), dt) for w, dt in _IN_SPLITS]
    kern = functools.partial(_ffn_kernel, ff_chunk=ff_chunk, n_chunks=dff // ff_chunk,
                             mix_in=mix is not None, proj_out=proj is not None)
    outs = pl.pallas_call(
        kern,
        grid=(n // tm,),
        in_specs=specs,
        out_specs=out_specs,
        out_shape=out_shape,
        scratch_shapes=[pltpu.VMEM((tm, d), F32)],
        compiler_params=_params("parallel"),
    )(*args)
    return outs if proj is not None else outs[0]


ATT_BLOCKS = 2


def _band_bias(rel_bias):
    qi = jnp.arange(WINDOW)[:, None]
    kj = jnp.arange(2 * WINDOW)[None, :]
    rel = qi + WINDOW - kj
    in_window = (rel >= 0) & (rel < WINDOW)
    n = jnp.maximum(rel, 0)
    max_exact = N_BUCKETS // 2
    nf = jnp.maximum(n, 1).astype(F32)
    large = max_exact + (jnp.log(nf / max_exact) / math.log(MAX_DISTANCE / max_exact)
                         * (N_BUCKETS - max_exact)).astype(jnp.int32)
    large = jnp.minimum(large, N_BUCKETS - 1)
    bucket = jnp.where(n < max_exact, n, large)
    onehot = (bucket[..., None] == jnp.arange(N_BUCKETS)).astype(F32)
    bias = jnp.einsum('qkb,bh->hqk', onehot, rel_bias.astype(F32), precision=lax.Precision.HIGHEST)
    return jnp.transpose(jnp.where(in_window[None], bias, MASK_VALUE), (0, 2, 1))


def _attn_kernel(sink_ref, q_ref, kp_ref, kc_ref, vp_ref, vc_ref, bias_ref, o_ref):
    first = pl.program_id(1) == 0
    q = q_ref[...] * (HEAD_DIM ** -0.5)
    k = jnp.concatenate([kp_ref[...], kc_ref[...]], axis=0)
    v = jnp.concatenate([vp_ref[...], vc_ref[...]], axis=0)
    row = lax.broadcasted_iota(jnp.int32, (1, 2 * WINDOW, 1), 1)
    has_prev = jnp.logical_or(jnp.logical_not(first), row >= WINDOW)
    sink = sink_ref[...]
    kv_of = [h // ATT_GROUP for h in range(ATT_HEADS)]
    head = lambda x, j: x[:, j * HEAD_DIM:(j + 1) * HEAD_DIM]
    blocks = []
    for i in range(ATT_BLOCKS):
        band = slice(i * WINDOW, (i + 2) * WINDOW)
        qi = q[i * WINDOW:(i + 1) * WINDOW]
        v_t = v[band].astype(F32).T.astype(BF16)
        qs = jnp.stack([head(qi, h) for h in range(ATT_HEADS)], axis=0)
        ks = jnp.stack([head(k[band], j) for j in kv_of], axis=0)
        vs = jnp.stack([v_t[j * HEAD_DIM:(j + 1) * HEAD_DIM] for j in kv_of], axis=0)
        s = lax.dot_general(ks, qs, (((2,), (2,)), ((0,), (0,))), preferred_element_type=F32)
        s = s + bias_ref[...]
        if i == 0:
            s = jnp.where(has_prev, s, MASK_VALUE)
        m = jnp.maximum(jnp.max(s, axis=1, keepdims=True), sink)
        p = jnp.exp(s - m)
        denom = jnp.sum(p, axis=1, keepdims=True) + jnp.exp(sink - m)
        o = lax.dot_general(vs, p.astype(BF16), (((2,), (1,)), ((0,), (0,))), preferred_element_type=F32)
        o = o / denom
        blocks.append(o.reshape(ATT_DIM, WINDOW).T)
    o_ref[...] = jnp.concatenate(blocks, axis=0).astype(o_ref.dtype)


def _attn_call(q, k, v, bias, sinks):
    bsz, t, _ = q.shape
    span = ATT_BLOCKS * WINDOW
    cur = lambda b, n: (b, n, 0)
    prv = lambda b, n: (b, jnp.maximum(n * ATT_BLOCKS - 1, 0), 0)
    cur_kv = pl.BlockSpec((None, span, ATT_KV_DIM), cur)
    prv_kv = pl.BlockSpec((None, WINDOW, ATT_KV_DIM), prv)
    return pl.pallas_call(
        _attn_kernel,
        grid=(bsz, t // span),
        in_specs=[
            _const_spec((ATT_HEADS, 1, 1)),
            pl.BlockSpec((None, span, ATT_DIM), cur),
            prv_kv, cur_kv, prv_kv, cur_kv,
            _const_spec((ATT_HEADS, 2 * WINDOW, WINDOW)),
        ],
        out_specs=pl.BlockSpec((None, span, ATT_DIM), cur),
        out_shape=jax.ShapeDtypeStruct((bsz, t, ATT_DIM), BF16),
        compiler_params=_params("parallel", "parallel"),
    )(sinks.reshape(ATT_HEADS, 1, 1), q, k, k, v, v, bias)


RWKV_PAIR = 2
RWKV_GROUP_HEADS = 4
RWKV_GROUP_W = RWKV_GROUP_HEADS * HEAD_DIM
RWKV_GROUPS = RWKV_PAIR * RWKV_HEADS // RWKV_GROUP_HEADS


def _gdot(a, b, contract):
    return lax.dot_general(a.astype(BF16), b.astype(BF16), (contract, ((0,), (0,))),
                           preferred_element_type=F32)


def _gmm(a, b):
    return _gdot(a, b, ((2,), (1,)))


def _gmm_nt(a, b):
    return _gdot(a, b, ((2,), (2,)))


def _gmm_tn(a, b):
    return _gdot(a, b, ((1,), (1,)))


def _flat_groups(x):
    return jnp.stack([x[:, j * RWKV_GROUP_W:(j + 1) * RWKV_GROUP_W] for j in range(RWKV_GROUPS)], axis=0)


def _head_rows(x):
    lane_head = lax.broadcasted_iota(jnp.int32, (1, RWKV_GROUP_W), 1) // HEAD_DIM
    zero = jnp.zeros((), x.dtype)
    groups = []
    for j in range(RWKV_GROUPS):
        xg = x[:, j * RWKV_GROUP_W:(j + 1) * RWKV_GROUP_W]
        groups.append(jnp.concatenate([jnp.where(lane_head == i, xg, zero) for i in range(RWKV_GROUP_HEADS)], axis=0))
    return jnp.stack(groups, axis=0)


def _rwkv_kernel(p_ref, mu_ref, w0_ref, a0_ref, kk_ref, ka_ref, rk_ref, gnw_ref, gnb_ref,
                 lora_ref, hb_ref, tril_ref, o_ref, prev_ref, s_ref, *, n_chunks):
    C, D, W = RWKV_CHUNK, RWKV_DIM, RWKV_GROUP_W
    tt = n_chunks * C

    @pl.when(pl.program_id(1) == 0)
    def _():
        prev_ref[...] = jnp.zeros_like(prev_ref)
        s_ref[...] = jnp.zeros_like(s_ref)

    p = p_ref[...].reshape(RWKV_PAIR * tt, RWKV_COLS)
    row = lax.broadcasted_iota(jnp.int32, p.shape, 0)
    prev = pltpu.roll(p, 1, 0)
    for i in range(RWKV_PAIR):
        prev = jnp.where(row == i * tt, prev_ref[i], prev)
        prev_ref[i] = p[(i + 1) * tt - 1:(i + 1) * tt, :]
    xs = p + (prev - p) * mu_ref[...]
    r, k, v = xs[:, 0:D], xs[:, D:2 * D], xs[:, 2 * D:3 * D]
    lo = xs[:, 3 * D:3 * D + LORA_COLS]
    lane = lax.broadcasted_iota(jnp.int32, lo.shape, 1)
    act = jnp.where(lane < DECAY_LORA, jnp.tanh(lo),
                    jnp.where(lane < DECAY_LORA + ICLR_LORA, lo, jax.nn.sigmoid(lo)))
    proj = jnp.dot(act.astype(BF16), lora_ref[...], preferred_element_type=F32)
    z = -(w0_ref[...] + proj[:, 0:D])
    softplus = jnp.maximum(z, 0.0) + jnp.log(1.0 + jnp.exp(-jnp.abs(z)))
    logw = -jnp.exp(-softplus - 0.5)
    a = jax.nn.sigmoid(a0_ref[...] + proj[:, D:2 * D])
    g = proj[:, 2 * D:3 * D]
    hb = hb_ref[...]
    kk = k * kk_ref[...]
    ss = jnp.dot((kk * kk).astype(BF16), hb, preferred_element_type=F32)
    kk = kk / jnp.maximum(jnp.sqrt(ss), 1e-12)
    k = k * (1.0 + (a - 1.0) * ka_ref[...])

    logw_hi = logw.astype(BF16)
    logw_lo = (logw - logw_hi.astype(F32)).astype(BF16)
    cum = (jnp.dot(tril_ref[...], logw_hi, preferred_element_type=F32)
           + jnp.dot(tril_ref[...], logw_lo, preferred_element_type=F32))
    e_pos = jnp.exp(cum)
    e_neg = jnp.exp(-cum)
    kb = kk * a
    r_t = (r * e_pos).astype(BF16)
    k_t = (k * e_neg).astype(BF16)
    b_t = (kb * e_neg).astype(BF16)
    a_t = (-kk * jnp.exp(cum - logw)).astype(BF16)
    v_b = v.astype(BF16)

    r64 = lax.broadcasted_iota(jnp.int32, (RWKV_GROUP_HEADS * C, W), 0) % C
    c64 = lax.broadcasted_iota(jnp.int32, (RWKV_GROUP_HEADS * C, W), 1) % C
    strict = (r64 > c64)[None]
    eye = (lax.broadcasted_iota(jnp.int32, (RWKV_GROUP_HEADS * C, W), 0)
           == lax.broadcasted_iota(jnp.int32, (RWKV_GROUP_HEADS * C, W), 1))[None].astype(F32)
    incl = (lax.broadcasted_iota(jnp.int32, (C, W), 0) >= lax.broadcasted_iota(jnp.int32, (C, W), 1) % C)[None]

    def pair_lanes(x, c):
        return jnp.concatenate([x[i * tt + c * C:i * tt + (c + 1) * C] for i in range(RWKV_PAIR)], axis=1)

    ys = []
    for c in range(n_chunks):
        g_end = pair_lanes(e_pos, c)[C - 1:C, :]
        k_end = (pair_lanes(k * e_neg, c) * g_end).astype(BF16)
        b_end = (pair_lanes(kb * e_neg, c) * g_end).astype(BF16)
        ax, bx, kx = _head_rows(pair_lanes(a_t, c)), _head_rows(pair_lanes(b_t, c)), _head_rows(pair_lanes(k_t, c))
        vx = _head_rows(pair_lanes(v_b, c))
        rf = _flat_groups(pair_lanes(r_t, c))
        sc = _gmm_nt(jnp.concatenate([ax, rf], axis=1), jnp.concatenate([bx, kx], axis=1))
        a_ab = jnp.where(strict, sc[:, :4 * C, :W], 0.0)
        a_ak = jnp.where(strict, sc[:, :4 * C, W:], 0.0)
        a_rb = jnp.where(incl, sc[:, 4 * C:, :W], 0.0)
        a_rk = jnp.where(incl, sc[:, 4 * C:, W:], 0.0)
        inv = eye + a_ab
        powr = _gmm(a_ab, a_ab)
        span = 2
        while span < C:
            if 2 * span < C:
                both = _gmm(powr, jnp.concatenate([inv.astype(BF16), powr.astype(BF16)], axis=-1))
                inv, powr = inv + both[..., :W], both[..., W:]
            else:
                inv = inv + _gmm(powr, inv)
            span *= 2
        akv = _gmm(a_ak, vx)

        s = s_ref[...]
        sr = _gmm_nt(jnp.concatenate([ax, rf], axis=1), s)
        u = _gmm(inv, sr[:, :4 * C] + akv)
        uv = jnp.concatenate([u.astype(BF16), vx], axis=1)
        y = sr[:, 4 * C:] + _gmm(jnp.concatenate([a_rb, a_rk], axis=-1), uv)
        s_ref[...] = (s * _flat_groups(g_end)
                      + _gmm_tn(uv, jnp.concatenate([_head_rows(b_end), _head_rows(k_end)], axis=1)))
        ys.append(jnp.concatenate([y[j] for j in range(RWKV_GROUPS)], axis=-1))

    y = jnp.concatenate([yc[:, i * D:(i + 1) * D] for i in range(RWKV_PAIR) for yc in ys], axis=0)
    inv_n = 1.0 / HEAD_DIM
    mean = jnp.dot(y.astype(BF16), hb, preferred_element_type=F32) * inv_n
    yc = y - mean
    var = jnp.dot((yc * yc).astype(BF16), hb, preferred_element_type=F32) * inv_n
    y = yc * lax.rsqrt(var + GN_EPS) * gnw_ref[...] + gnb_ref[...]
    rk = jnp.dot((r * k * rk_ref[...]).astype(BF16), hb, preferred_element_type=F32)
    o_ref[...] = ((y + rk * v) * g).reshape(RWKV_PAIR, tt, D).astype(o_ref.dtype)


def _rwkv_call(rw, mu, w0, w2, a0, a2, g2, k_k, k_a, r_k, gn_w, gn_b, *, n_chunks):
    bsz, t, cols = rw.shape
    C, D = RWKV_CHUNK, RWKV_DIM
    tt = n_chunks * C
    lora = jnp.zeros((LORA_COLS, 3 * D), F32)
    lora = lora.at[0:DECAY_LORA, 0:D].set(w2)
    lora = lora.at[DECAY_LORA:DECAY_LORA + ICLR_LORA, D:2 * D].set(a2)
    lora = lora.at[DECAY_LORA + ICLR_LORA:, 2 * D:].set(g2)
    head_id = jnp.arange(D) // HEAD_DIM
    hb = (head_id[:, None] == head_id[None, :]).astype(BF16)
    pos = jnp.arange(RWKV_PAIR * tt)
    tril = ((pos[:, None] >= pos[None, :]) & (pos[:, None] // C == pos[None, :] // C)).astype(BF16)
    vec = lambda x: x.reshape(1, -1).astype(F32)
    vecs = [vec(mu), vec(w0), vec(a0), vec(k_k), vec(k_a), vec(r_k), vec(gn_w), vec(gn_b)]
    consts = vecs + [lora.astype(BF16), hb, tril]
    kern = functools.partial(_rwkv_kernel, n_chunks=n_chunks)
    return pl.pallas_call(
        kern,
        grid=(bsz // RWKV_PAIR, t // tt),
        in_specs=[pl.BlockSpec((RWKV_PAIR, tt, cols), lambda b, c: (b, c, 0))] + [_const_spec(x.shape) for x in consts],
        out_specs=pl.BlockSpec((RWKV_PAIR, tt, D), lambda b, c: (b, c, 0)),
        out_shape=jax.ShapeDtypeStruct((bsz, t, D), BF16),
        scratch_shapes=[pltpu.VMEM((RWKV_PAIR, 1, cols), F32),
                        pltpu.VMEM((RWKV_GROUPS, RWKV_GROUP_W, RWKV_GROUP_W), F32)],
        compiler_params=_params("parallel", "arbitrary"),
    )(rw, *consts)


SSM_TILE_GROUPS = 2
SSM_TILES = SSM_GROUPS // SSM_TILE_GROUPS
SSM_HALF = SSM_TILE_GROUPS * SSM_STATE
SSM_LANE_TILE = SSM_DIM // 2


def _s5_operators(lam_re, lam_im, log_dt, b_re, b_im, c_re, c_im, n_levels):
    L, G, P, H = SSM_CHUNK, SSM_GROUPS, SSM_STATE, SSM_GROUP_CH
    hi = lax.Precision.HIGHEST
    lr = jnp.minimum(lam_re.astype(F32), -1e-4)
    li = lam_im.astype(F32)
    dt = jnp.exp(log_dt.astype(F32))[:, None]
    n = jnp.arange(L + 1, dtype=F32)[:, None, None]
    mag = jnp.exp(lr * dt * n)
    pr = mag * jnp.cos(li * dt * n)
    pi = mag * jnp.sin(li * dt * n)
    den = lr * lr + li * li
    nr, ni = pr[1] - 1.0, pi[1]
    fr = (nr * lr + ni * li) / den
    fi = (ni * lr - nr * li) / den
    b_re, b_im = b_re.astype(F32), b_im.astype(F32)
    bb_re = fr[..., None] * b_re - fi[..., None] * b_im
    bb_im = fr[..., None] * b_im + fi[..., None] * b_re
    c_re, c_im = c_re.astype(F32), c_im.astype(F32)
    cl_re = c_re[None] * pr[:L, :, None, :] - c_im[None] * pi[:L, :, None, :]
    cl_im = c_re[None] * pi[:L, :, None, :] + c_im[None] * pr[:L, :, None, :]
    kern = (jnp.einsum('nghp,gpk->nghk', cl_re, bb_re, precision=hi)
            - jnp.einsum('nghp,gpk->nghk', cl_im, bb_im, precision=hi))
    eye = jnp.eye(G, dtype=F32)
    k_lag = jnp.einsum('lgab,gk->lgbka', kern, eye).reshape(L, G * H, G * H)

    flat = lambda x: x.reshape(x.shape[0], 1, G * P)
    b_fr = jnp.einsum('gph,gk->ghkp', bb_re, eye).reshape(1, G * H, G * P)
    b_fi = jnp.einsum('gph,gk->ghkp', bb_im, eye).reshape(1, G * H, G * P)
    rev_r, rev_i = flat(pr[:L][::-1]), flat(pi[:L][::-1])
    tile_cols = lambda m: m.reshape(L, G * H, SSM_TILES, SSM_HALF)
    b_end = jnp.concatenate([tile_cols(b_fr * rev_r - b_fi * rev_i), tile_cols(b_fi * rev_r + b_fr * rev_i)],
                            axis=-1).reshape(L * G * H, SSM_TILES * 2 * SSM_HALF)
    c_fr = jnp.einsum('ghp,gk->gpkh', c_re, eye).reshape(1, G * P, G * H)
    c_fi = jnp.einsum('ghp,gk->gpkh', c_im, eye).reshape(1, G * P, G * H)
    fwd_r = pr[1:].reshape(L, G * P, 1)
    fwd_i = pi[1:].reshape(L, G * P, 1)
    tile_rows = lambda m: m.reshape(L, SSM_TILES, SSM_HALF, G * H)
    c_end = jnp.concatenate([tile_rows(c_fr * fwd_r - c_fi * fwd_i), -tile_rows(c_fr * fwd_i + c_fi * fwd_r)],
                            axis=2)
    c_end = jnp.transpose(c_end, (1, 2, 0, 3)).reshape(SSM_TILES * 2 * SSM_HALF, L * G * H)
    lam = lambda x: x.reshape(SSM_TILES, 1, SSM_HALF)
    qr, qi = pr[L], pi[L]
    scan_r, scan_i = [], []
    for _ in range(n_levels):
        scan_r.append(lam(qr))
        scan_i.append(lam(qi))
        qr, qi = qr * qr - qi * qi, 2.0 * qr * qi
    return (k_lag.astype(BF16), b_end.astype(BF16), c_end.astype(BF16), jnp.stack(scan_r), jnp.stack(scan_i))


def _cmul(x, cr, ci):
    re, im = x[:, :SSM_HALF], x[:, SSM_HALF:]
    return jnp.concatenate([cr * re - ci * im, cr * im + ci * re], axis=1)


def _s5_kernel(ulo_ref, uhi_ref, klag_ref, bend_ref, cend_ref, scanr_ref, scani_ref, d_ref,
               gluw_ref, glub_ref, olo_ref, ohi_ref, utb_ref, ucm_ref, acc_ref, xin_ref, *, n_chunks, n_levels):
    L, nc, W = SSM_CHUNK, n_chunks, SSM_DIM
    tile_w = 2 * SSM_HALF
    for s in range(L):
        rows = jnp.concatenate([ref[pl.ds(s, nc, stride=L), :] for ref in (ulo_ref, uhi_ref)], axis=1)
        acc_ref[s * nc:(s + 1) * nc, :] = d_ref[...] * rows
        utb_ref[s * nc:(s + 1) * nc, :] = rows.astype(BF16)
        ucm_ref[:, s * W:(s + 1) * W] = rows.astype(BF16)
    for lag in range(L):
        acc_ref[lag * nc:, :] += jnp.dot(utb_ref[0:(L - lag) * nc, :], klag_ref[lag], preferred_element_type=F32)
    e = jnp.dot(ucm_ref[...], bend_ref[...], preferred_element_type=F32)
    row = lax.broadcasted_iota(jnp.int32, (nc, tile_w), 0)
    for j in range(SSM_TILES):
        x = e[:, j * tile_w:(j + 1) * tile_w]
        for k in range(n_levels):
            sh = jnp.where(row >= (1 << k), pltpu.roll(x, 1 << k, 0), 0.0)
            x = x + _cmul(sh, scanr_ref[k, j], scani_ref[k, j])
        xin = jnp.where(row >= 1, pltpu.roll(x, 1, 0), 0.0)
        xin_ref[:, j * tile_w:(j + 1) * tile_w] = xin.astype(BF16)
    ys = jnp.dot(xin_ref[...], cend_ref[...], preferred_element_type=F32)
    for t in range(L):
        y = jax.nn.gelu(acc_ref[t * nc:(t + 1) * nc, :] + ys[:, t * W:(t + 1) * W])
        gate = jnp.dot(y.astype(BF16), gluw_ref[...], preferred_element_type=F32) + glub_ref[...]
        y = y * jax.nn.sigmoid(gate)
        olo_ref[pl.ds(t, nc, stride=L), :] = y[:, :SSM_LANE_TILE]
        ohi_ref[pl.ds(t, nc, stride=L), :] = y[:, SSM_LANE_TILE:]


def _s5_call(su_lo, su_hi, ops, d, glu_w, glu_b):
    bsz, t, half = su_lo.shape
    width = 2 * half
    nc = t // SSM_CHUNK
    n_levels = ops[-1].shape[0]
    consts = list(ops) + [d.reshape(1, width).astype(F32), glu_w, glu_b.reshape(1, width).astype(F32)]
    kern = functools.partial(_s5_kernel, n_chunks=nc, n_levels=n_levels)
    seq = pl.BlockSpec((None, t, half), lambda b: (b, 0, 0))
    return pl.pallas_call(
        kern,
        grid=(bsz,),
        in_specs=[seq, seq] + [_const_spec(x.shape) for x in consts],
        out_specs=[seq, seq],
        out_shape=[jax.ShapeDtypeStruct((bsz, t, half), F32)] * 2,
        scratch_shapes=[pltpu.VMEM((t, width), BF16), pltpu.VMEM((nc, SSM_CHUNK * width), BF16),
                        pltpu.VMEM((t, width), F32), pltpu.VMEM((nc, SSM_TILES * 2 * SSM_HALF), BF16)],
        compiler_params=_params("parallel"),
    )(su_lo, su_hi, *consts)


def kernel(x, rel_bias, ln_pre_ffn1, ln_post_ffn1, ffn1_w_gate, ffn1_w_up, ffn1_w_down, ln_pre_mix, ln_post_mix, w_in, w_out, att_sinks, rwkv_mu, rwkv_w0, rwkv_w2, rwkv_a0, rwkv_a2, rwkv_g2, rwkv_k_k, rwkv_k_a, rwkv_r_k, rwkv_gn_w, rwkv_gn_b, ssm_lambda_re, ssm_lambda_im, ssm_log_dt, ssm_b_re, ssm_b_im, ssm_c_re, ssm_c_im, ssm_d, ssm_glu_w, ssm_glu_b, ln_pre_ffn2, ln_post_ffn2, ffn2_w_gate, ffn2_w_up, ffn2_w_down):
    bsz, t, d = x.shape
    n = bsz * t
    depth = w_in.shape[0]
    tm = 512 if n % 512 == 0 else 128
    ff_chunk = 256
    rwkv_chunks = max(c for c in (4, 2, 1) if t % (c * RWKV_CHUNK) == 0)
    bf = lambda w: w.astype(BF16)
    bias = _band_bias(rel_bias)
    h = x.reshape(n, d)
    seq = lambda z: z.reshape(bsz, t, z.shape[-1])
    flat = lambda z: z.reshape(n, z.shape[-1])
    n_levels = max(1, math.ceil(math.log2(t // SSM_CHUNK)))
    for l in range(depth):
        h, q, k, v, rw, su_lo, su_hi = _ffn_call(
            h, ln_pre_ffn1[l], ln_post_ffn1[l], bf(ffn1_w_gate[l]), bf(ffn1_w_up[l]), bf(ffn1_w_down[l]),
            tm=tm, ff_chunk=ff_chunk, proj=(ln_pre_mix[l], bf(w_in[l])))
        ya = _attn_call(seq(q), seq(k), seq(v), bias, att_sinks[l].astype(F32))
        yr = _rwkv_call(seq(rw), rwkv_mu[l], rwkv_w0[l], rwkv_w2[l], rwkv_a0[l], rwkv_a2[l], rwkv_g2[l],
                        rwkv_k_k[l], rwkv_k_a[l], rwkv_r_k[l], rwkv_gn_w[l], rwkv_gn_b[l], n_chunks=rwkv_chunks)
        ops = _s5_operators(ssm_lambda_re[l], ssm_lambda_im[l], ssm_log_dt[l], ssm_b_re[l], ssm_b_im[l],
                            ssm_c_re[l], ssm_c_im[l], n_levels=n_levels)
        ys_lo, ys_hi = _s5_call(seq(su_lo), seq(su_hi), ops, ssm_d[l], bf(ssm_glu_w[l]), ssm_glu_b[l])
        h = _ffn_call(h, ln_pre_ffn2[l], ln_post_ffn2[l], bf(ffn2_w_gate[l]), bf(ffn2_w_up[l]), bf(ffn2_w_down[l]),
                      tm=tm, ff_chunk=ff_chunk,
                      mix=([flat(ya), flat(yr), flat(ys_lo), flat(ys_hi)], bf(w_out[l]), ln_post_mix[l]))
    return h.reshape(bsz, t, d)
```

```python
import functools
import math

import jax
import jax.numpy as jnp
from jax import lax
from jax.experimental import pallas as pl
from jax.experimental.pallas import tpu as pltpu

F32 = jnp.float32
BF16 = jnp.bfloat16

HEAD_DIM = 64
ATT_HEADS = 6
ATT_KV_HEADS = 2
ATT_GROUP = ATT_HEADS // ATT_KV_HEADS
ATT_DIM = ATT_HEADS * HEAD_DIM
ATT_KV_DIM = ATT_KV_HEADS * HEAD_DIM
WINDOW = 128
N_BUCKETS = 32
MAX_DISTANCE = 128
RWKV_HEADS = 6
RWKV_DIM = RWKV_HEADS * HEAD_DIM
DECAY_LORA = 32
ICLR_LORA = 32
GATE_LORA = 64
LORA_COLS = DECAY_LORA + ICLR_LORA + GATE_LORA
RWKV_COLS = 3 * RWKV_DIM + LORA_COLS
GN_EPS = 64e-5
SSM_GROUPS = 16
SSM_GROUP_CH = 16
SSM_DIM = SSM_GROUPS * SSM_GROUP_CH
SSM_STATE = 64
RMS_EPS = 1e-6
MASK_VALUE = -1e30

VMEM_LIMIT_BYTES = 56 * 1024 * 1024
RWKV_CHUNK = 64
SSM_CHUNK = 8


def _params(*sem):
    return pltpu.CompilerParams(dimension_semantics=sem, vmem_limit_bytes=VMEM_LIMIT_BYTES)


def _rms(x, g):
    return x * lax.rsqrt(jnp.mean(x * x, axis=-1, keepdims=True) + RMS_EPS) * g


def _const_spec(shape):
    nd = len(shape)
    return pl.BlockSpec(shape, lambda *_: (0,) * nd, pipeline_mode=pl.Buffered(1))


def _layer_spec(stacked, layer):
    nd = stacked.ndim - 1
    return pl.BlockSpec((None,) + stacked.shape[1:], lambda *_: (layer,) + (0,) * nd, pipeline_mode=pl.Buffered(1))


_IN_SPLITS = ((ATT_DIM, BF16), (ATT_KV_DIM, BF16), (ATT_KV_DIM, BF16), (RWKV_COLS, F32),
              (SSM_DIM // 2, F32), (SSM_DIM // 2, F32))
_N_MIX = 4


def _ffn_kernel(*refs, ff_chunk, n_chunks, mix_in, proj_out):
    it = iter(refs)
    x_ref = next(it)
    if mix_in:
        y_refs = [next(it) for _ in range(_N_MIX)]
        wout_ref, gmix_ref = next(it), next(it)
    gpre_ref, gpost_ref, wg_ref, wu_ref, wd_ref = (next(it) for _ in range(5))
    if proj_out:
        gin_ref, win_ref = next(it), next(it)
    o_ref = next(it)
    if proj_out:
        proj_refs = [next(it) for _ in _IN_SPLITS]
    acc_ref = next(it)

    x = x_ref[...]
    if mix_in:
        y = jnp.concatenate([r[...].astype(BF16) for r in y_refs], axis=-1)
        x = x + _rms(jnp.dot(y, wout_ref[...], preferred_element_type=F32), gmix_ref[...])
    xn = _rms(x, gpre_ref[...]).astype(BF16)
    for c in range(n_chunks):
        sl = slice(c * ff_chunk, (c + 1) * ff_chunk)
        g = jnp.dot(xn, wg_ref[:, sl], preferred_element_type=F32)
        u = jnp.dot(xn, wu_ref[:, sl], preferred_element_type=F32)
        a = (g * jax.nn.sigmoid(g) * u).astype(BF16)
        d = jnp.dot(a, wd_ref[sl, :], preferred_element_type=F32)
        if c == 0:
            acc_ref[...] = d
        else:
            acc_ref[...] += d
    out = x + 0.5 * _rms(acc_ref[...], gpost_ref[...])
    o_ref[...] = out
    if proj_out:
        p = jnp.dot(_rms(out, gin_ref[...]).astype(BF16), win_ref[...], preferred_element_type=F32)
        off = 0
        for ref, (width, dtype) in zip(proj_refs, _IN_SPLITS):
            ref[...] = p[:, off:off + width].astype(dtype)
            off += width


def _ffn_call(h, layer, g_pre, g_post, wg, wu, wd, *, tm, ff_chunk, mix=None, proj=None):
    n, d = h.shape
    dff = wg.shape[-1]
    row = lambda w: pl.BlockSpec((tm, w), lambda i: (i, 0))
    vec = lambda g: g.reshape(g.shape[0], 1, d)
    args, specs = [h], [row(d)]
    if mix is not None:
        ys, w_out, g_mix = mix
        args += list(ys) + [w_out, vec(g_mix)]
        specs += [row(y.shape[-1]) for y in ys]
    args += [vec(g_pre), vec(g_post), wg, wu, wd]
    if proj is not None:
        g_in, w_in = proj
        args += [vec(g_in), w_in]
    specs += [_layer_spec(a, layer) for a in args[len(specs):]]
    out_specs, out_shape = [row(d)], [jax.ShapeDtypeStruct((n, d), F32)]
    if proj is not None:
        out_specs += [row(w) for w, _ in _IN_SPLITS]
        out_shape += [jax.ShapeDtypeStruct((n, w), dt) for w, dt in _IN_SPLITS]
    kern = functools.partial(_ffn_kernel, ff_chunk=ff_chunk, n_chunks=dff // ff_chunk,
                             mix_in=mix is not None, proj_out=proj is not None)
    outs = pl.pallas_call(
        kern,
        grid=(n // tm,),
        in_specs=specs,
        out_specs=out_specs,
        out_shape=out_shape,
        scratch_shapes=[pltpu.VMEM((tm, d), F32)],
        compiler_params=_params("parallel"),
    )(*args)
    return outs if proj is not None else outs[0]


ATT_BLOCKS = 2


def _band_bias(rel_bias):
    qi = jnp.arange(WINDOW)[:, None]
    kj = jnp.arange(2 * WINDOW)[None, :]
    rel = qi + WINDOW - kj
    in_window = (rel >= 0) & (rel < WINDOW)
    n = jnp.maximum(rel, 0)
    max_exact = N_BUCKETS // 2
    nf = jnp.maximum(n, 1).astype(F32)
    large = max_exact + (jnp.log(nf / max_exact) / math.log(MAX_DISTANCE / max_exact)
                         * (N_BUCKETS - max_exact)).astype(jnp.int32)
    large = jnp.minimum(large, N_BUCKETS - 1)
    bucket = jnp.where(n < max_exact, n, large)
    onehot = (bucket[..., None] == jnp.arange(N_BUCKETS)).astype(F32)
    bias = jnp.einsum('qkb,bh->hqk', onehot, rel_bias.astype(F32), precision=lax.Precision.HIGHEST)
    return jnp.transpose(jnp.where(in_window[None], bias, MASK_VALUE), (0, 2, 1))


def _attn_kernel(sink_ref, q_ref, kp_ref, kc_ref, vp_ref, vc_ref, bias_ref, o_ref):
    first = pl.program_id(1) == 0
    q = q_ref[...] * (HEAD_DIM ** -0.5)
    k = jnp.concatenate([kp_ref[...], kc_ref[...]], axis=0)
    v = jnp.concatenate([vp_ref[...], vc_ref[...]], axis=0)
    row = lax.broadcasted_iota(jnp.int32, (1, 2 * WINDOW, 1), 1)
    has_prev = jnp.logical_or(jnp.logical_not(first), row >= WINDOW)
    sink = sink_ref[...]
    kv_of = [h // ATT_GROUP for h in range(ATT_HEADS)]
    head = lambda x, j: x[:, j * HEAD_DIM:(j + 1) * HEAD_DIM]
    blocks = []
    for i in range(ATT_BLOCKS):
        band = slice(i * WINDOW, (i + 2) * WINDOW)
        qi = q[i * WINDOW:(i + 1) * WINDOW]
        v_t = v[band].astype(F32).T.astype(BF16)
        qs = jnp.stack([head(qi, h) for h in range(ATT_HEADS)], axis=0)
        ks = jnp.stack([head(k[band], j) for j in kv_of], axis=0)
        vs = jnp.stack([v_t[j * HEAD_DIM:(j + 1) * HEAD_DIM] for j in kv_of], axis=0)
        s = lax.dot_general(ks, qs, (((2,), (2,)), ((0,), (0,))), preferred_element_type=F32)
        s = s + bias_ref[...]
        if i == 0:
            s = jnp.where(has_prev, s, MASK_VALUE)
        m = jnp.maximum(jnp.max(s, axis=1, keepdims=True), sink)
        p = jnp.exp(s - m)
        denom = jnp.sum(p, axis=1, keepdims=True) + jnp.exp(sink - m)
        o = lax.dot_general(vs, p.astype(BF16), (((2,), (1,)), ((0,), (0,))), preferred_element_type=F32)
        o = o / denom
        blocks.append(o.reshape(ATT_DIM, WINDOW).T)
    o_ref[...] = jnp.concatenate(blocks, axis=0).astype(o_ref.dtype)


def _attn_call(q, k, v, bias, sinks):
    bsz, t, _ = q.shape
    span = ATT_BLOCKS * WINDOW
    cur = lambda b, n: (b, n, 0)
    prv = lambda b, n: (b, jnp.maximum(n * ATT_BLOCKS - 1, 0), 0)
    cur_kv = pl.BlockSpec((None, span, ATT_KV_DIM), cur)
    prv_kv = pl.BlockSpec((None, WINDOW, ATT_KV_DIM), prv)
    return pl.pallas_call(
        _attn_kernel,
        grid=(bsz, t // span),
        in_specs=[
            _const_spec((ATT_HEADS, 1, 1)),
            pl.BlockSpec((None, span, ATT_DIM), cur),
            prv_kv, cur_kv, prv_kv, cur_kv,
            _const_spec((ATT_HEADS, 2 * WINDOW, WINDOW)),
        ],
        out_specs=pl.BlockSpec((None, span, ATT_DIM), cur),
        out_shape=jax.ShapeDtypeStruct((bsz, t, ATT_DIM), BF16),
        compiler_params=_params("parallel", "parallel"),
    )(sinks.reshape(ATT_HEADS, 1, 1), q, k, k, v, v, bias)


RWKV_PAIR = 2
RWKV_GROUP_HEADS = 4
RWKV_GROUP_W = RWKV_GROUP_HEADS * HEAD_DIM
RWKV_GROUPS = RWKV_PAIR * RWKV_HEADS // RWKV_GROUP_HEADS


def _gdot(a, b, contract):
    return lax.dot_general(a.astype(BF16), b.astype(BF16), (contract, ((0,), (0,))),
                           preferred_element_type=F32)


def _gmm(a, b):
    return _gdot(a, b, ((2,), (1,)))


def _gmm_nt(a, b):
    return _gdot(a, b, ((2,), (2,)))


def _gmm_tn(a, b):
    return _gdot(a, b, ((1,), (1,)))


def _flat_groups(x):
    return jnp.stack([x[:, j * RWKV_GROUP_W:(j + 1) * RWKV_GROUP_W] for j in range(RWKV_GROUPS)], axis=0)


def _head_rows(x):
    lane_head = lax.broadcasted_iota(jnp.int32, (1, RWKV_GROUP_W), 1) // HEAD_DIM
    zero = jnp.zeros((), x.dtype)
    groups = []
    for j in range(RWKV_GROUPS):
        xg = x[:, j * RWKV_GROUP_W:(j + 1) * RWKV_GROUP_W]
        groups.append(jnp.concatenate([jnp.where(lane_head == i, xg, zero) for i in range(RWKV_GROUP_HEADS)], axis=0))
    return jnp.stack(groups, axis=0)


def _rwkv_kernel(p_ref, mu_ref, w0_ref, a0_ref, kk_ref, ka_ref, rk_ref, gnw_ref, gnb_ref,
                 lora_ref, hb_ref, tril_ref, o_ref, prev_ref, s_ref, *, n_chunks):
    C, D, W = RWKV_CHUNK, RWKV_DIM, RWKV_GROUP_W
    tt = n_chunks * C

    @pl.when(pl.program_id(1) == 0)
    def _():
        prev_ref[...] = jnp.zeros_like(prev_ref)
        s_ref[...] = jnp.zeros_like(s_ref)

    p = p_ref[...].reshape(RWKV_PAIR * tt, RWKV_COLS)
    row = lax.broadcasted_iota(jnp.int32, p.shape, 0)
    prev = pltpu.roll(p, 1, 0)
    for i in range(RWKV_PAIR):
        prev = jnp.where(row == i * tt, prev_ref[i], prev)
        prev_ref[i] = p[(i + 1) * tt - 1:(i + 1) * tt, :]
    xs = p + (prev - p) * mu_ref[...]
    lo = xs[:, 3 * D:3 * D + LORA_COLS]
    lane = lax.broadcasted_iota(jnp.int32, lo.shape, 1)
    act = jnp.where(lane < DECAY_LORA, jnp.tanh(lo),
                    jnp.where(lane < DECAY_LORA + ICLR_LORA, lo, jax.nn.sigmoid(lo)))
    proj = jnp.dot(act.astype(BF16), lora_ref[...], preferred_element_type=F32)

    def pair(x):
        return jnp.concatenate([x[i * tt:(i + 1) * tt] for i in range(RWKV_PAIR)], axis=1)

    def head_sum(x):
        xb = x.astype(BF16)
        return jnp.concatenate([jnp.dot(xb[:, j * W:(j + 1) * W], hb_ref[...], preferred_element_type=F32)
                                for j in range(RWKV_GROUPS)], axis=1)

    r, k, v = pair(xs[:, 0:D]), pair(xs[:, D:2 * D]), pair(xs[:, 2 * D:3 * D])
    z = -(w0_ref[...] + pair(proj[:, 0:D]))
    softplus = jnp.maximum(z, 0.0) + jnp.log(1.0 + jnp.exp(-jnp.abs(z)))
    logw = -jnp.exp(-softplus - 0.5)
    a = jax.nn.sigmoid(a0_ref[...] + pair(proj[:, D:2 * D]))
    g = pair(proj[:, 2 * D:3 * D])
    kk = k * kk_ref[...]
    kk = kk / jnp.maximum(jnp.sqrt(head_sum(kk * kk)), 1e-12)
    k = k * (1.0 + (a - 1.0) * ka_ref[...])

    logw_hi = logw.astype(BF16)
    logw_lo = (logw - logw_hi.astype(F32)).astype(BF16)
    cum = (jnp.dot(tril_ref[...], logw_hi, preferred_element_type=F32)
           + jnp.dot(tril_ref[...], logw_lo, preferred_element_type=F32))
    e_pos = jnp.exp(cum)
    e_neg = jnp.exp(-cum)
    kb = kk * a
    r_t = (r * e_pos).astype(BF16)
    k_t = (k * e_neg).astype(BF16)
    b_t = (kb * e_neg).astype(BF16)
    a_t = (-kk * jnp.exp(cum - logw)).astype(BF16)
    v_b = v.astype(BF16)

    r64 = lax.broadcasted_iota(jnp.int32, (RWKV_GROUP_HEADS * C, W), 0) % C
    c64 = lax.broadcasted_iota(jnp.int32, (RWKV_GROUP_HEADS * C, W), 1) % C
    strict = (r64 > c64)[None]
    eye = (lax.broadcasted_iota(jnp.int32, (RWKV_GROUP_HEADS * C, W), 0)
           == lax.broadcasted_iota(jnp.int32, (RWKV_GROUP_HEADS * C, W), 1))[None].astype(F32)
    incl = (lax.broadcasted_iota(jnp.int32, (C, W), 0) >= lax.broadcasted_iota(jnp.int32, (C, W), 1) % C)[None]

    def pair_lanes(x, c):
        return x[c * C:(c + 1) * C]

    ys = []
    for c in range(n_chunks):
        g_end = pair_lanes(e_pos, c)[C - 1:C, :]
        k_end = (pair_lanes(k * e_neg, c) * g_end).astype(BF16)
        b_end = (pair_lanes(kb * e_neg, c) * g_end).astype(BF16)
        ax, bx, kx = _head_rows(pair_lanes(a_t, c)), _head_rows(pair_lanes(b_t, c)), _head_rows(pair_lanes(k_t, c))
        vx = _head_rows(pair_lanes(v_b, c))
        rf = _flat_groups(pair_lanes(r_t, c))
        sc = _gmm_nt(jnp.concatenate([ax, rf], axis=1), jnp.concatenate([bx, kx], axis=1))
        a_ab = jnp.where(strict, sc[:, :4 * C, :W], 0.0)
        a_ak = jnp.where(strict, sc[:, :4 * C, W:], 0.0)
        a_rb = jnp.where(incl, sc[:, 4 * C:, :W], 0.0)
        a_rk = jnp.where(incl, sc[:, 4 * C:, W:], 0.0)
        inv = eye + a_ab
        powr = _gmm(a_ab, a_ab)
        span = 2
        while span < C:
            if 2 * span < C:
                both = _gmm(powr, jnp.concatenate([inv.astype(BF16), powr.astype(BF16)], axis=-1))
                inv, powr = inv + both[..., :W], both[..., W:]
            else:
                inv = inv + _gmm(powr, inv)
            span *= 2
        akv = _gmm(a_ak, vx)

        s = s_ref[...]
        sr = _gmm_nt(jnp.concatenate([ax, rf], axis=1), s)
        u = _gmm(inv, sr[:, :4 * C] + akv)
        uv = jnp.concatenate([u.astype(BF16), vx], axis=1)
        y = sr[:, 4 * C:] + _gmm(jnp.concatenate([a_rb, a_rk], axis=-1), uv)
        s_ref[...] = (s * _flat_groups(g_end)
                      + _gmm_tn(uv, jnp.concatenate([_head_rows(b_end), _head_rows(k_end)], axis=1)))
        ys.append(jnp.concatenate([y[j] for j in range(RWKV_GROUPS)], axis=-1))

    y = jnp.concatenate(ys, axis=0)
    inv_n = 1.0 / HEAD_DIM
    yc = y - head_sum(y) * inv_n
    var = head_sum(yc * yc) * inv_n
    y = yc * lax.rsqrt(var + GN_EPS) * gnw_ref[...] + gnb_ref[...]
    out = (y + head_sum(r * k * rk_ref[...]) * v) * g
    for i in range(RWKV_PAIR):
        o_ref[i] = out[:, i * D:(i + 1) * D].astype(o_ref.dtype)


def _rwkv_call(rw, mu, w0, w2, a0, a2, g2, k_k, k_a, r_k, gn_w, gn_b, *, n_chunks):
    bsz, t, cols = rw.shape
    C, D = RWKV_CHUNK, RWKV_DIM
    tt = n_chunks * C
    lora = jnp.zeros((LORA_COLS, 3 * D), F32)
    lora = lora.at[0:DECAY_LORA, 0:D].set(w2)
    lora = lora.at[DECAY_LORA:DECAY_LORA + ICLR_LORA, D:2 * D].set(a2)
    lora = lora.at[DECAY_LORA + ICLR_LORA:, 2 * D:].set(g2)
    head_id = jnp.arange(RWKV_GROUP_W) // HEAD_DIM
    hb = (head_id[:, None] == head_id[None, :]).astype(BF16)
    pos = jnp.arange(tt)
    tril = ((pos[:, None] >= pos[None, :]) & (pos[:, None] // C == pos[None, :] // C)).astype(BF16)
    vec = lambda x: x.reshape(1, -1).astype(F32)
    both = lambda x: jnp.tile(vec(x), (1, RWKV_PAIR))
    vecs = [vec(mu), both(w0), both(a0), both(k_k), both(k_a), both(r_k), both(gn_w), both(gn_b)]
    consts = vecs + [lora.astype(BF16), hb, tril]
    kern = functools.partial(_rwkv_kernel, n_chunks=n_chunks)
    return pl.pallas_call(
        kern,
        grid=(bsz // RWKV_PAIR, t // tt),
        in_specs=[pl.BlockSpec((RWKV_PAIR, tt, cols), lambda b, c: (b, c, 0))] + [_const_spec(x.shape) for x in consts],
        out_specs=pl.BlockSpec((RWKV_PAIR, tt, D), lambda b, c: (b, c, 0)),
        out_shape=jax.ShapeDtypeStruct((bsz, t, D), BF16),
        scratch_shapes=[pltpu.VMEM((RWKV_PAIR, 1, cols), F32),
                        pltpu.VMEM((RWKV_GROUPS, RWKV_GROUP_W, RWKV_GROUP_W), F32)],
        compiler_params=_params("parallel", "arbitrary"),
    )(rw, *consts)


SSM_TILE_GROUPS = 2
SSM_TILES = SSM_GROUPS // SSM_TILE_GROUPS
SSM_HALF = SSM_TILE_GROUPS * SSM_STATE
SSM_LANE_TILE = SSM_DIM // 2


def _s5_operators(lam_re, lam_im, log_dt, b_re, b_im, c_re, c_im, n_levels):
    L, G, P, H = SSM_CHUNK, SSM_GROUPS, SSM_STATE, SSM_GROUP_CH
    hi = lax.Precision.HIGHEST
    lr = jnp.minimum(lam_re.astype(F32), -1e-4)
    li = lam_im.astype(F32)
    dt = jnp.exp(log_dt.astype(F32))[:, None]
    n = jnp.arange(L + 1, dtype=F32)[:, None, None]
    mag = jnp.exp(lr * dt * n)
    pr = mag * jnp.cos(li * dt * n)
    pi = mag * jnp.sin(li * dt * n)
    den = lr * lr + li * li
    nr, ni = pr[1] - 1.0, pi[1]
    fr = (nr * lr + ni * li) / den
    fi = (ni * lr - nr * li) / den
    b_re, b_im = b_re.astype(F32), b_im.astype(F32)
    bb_re = fr[..., None] * b_re - fi[..., None] * b_im
    bb_im = fr[..., None] * b_im + fi[..., None] * b_re
    c_re, c_im = c_re.astype(F32), c_im.astype(F32)
    cl_re = c_re[None] * pr[:L, :, None, :] - c_im[None] * pi[:L, :, None, :]
    cl_im = c_re[None] * pi[:L, :, None, :] + c_im[None] * pr[:L, :, None, :]
    kern = (jnp.einsum('nghp,gpk->nghk', cl_re, bb_re, precision=hi)
            - jnp.einsum('nghp,gpk->nghk', cl_im, bb_im, precision=hi))
    eye = jnp.eye(G, dtype=F32)
    k_lag = jnp.einsum('lgab,gk->lgbka', kern, eye).reshape(L, G * H, G * H)

    flat = lambda x: x.reshape(x.shape[0], 1, G * P)
    b_fr = jnp.einsum('gph,gk->ghkp', bb_re, eye).reshape(1, G * H, G * P)
    b_fi = jnp.einsum('gph,gk->ghkp', bb_im, eye).reshape(1, G * H, G * P)
    rev_r, rev_i = flat(pr[:L][::-1]), flat(pi[:L][::-1])
    tile_cols = lambda m: m.reshape(L, G * H, SSM_TILES, SSM_HALF)
    b_end = jnp.concatenate([tile_cols(b_fr * rev_r - b_fi * rev_i), tile_cols(b_fi * rev_r + b_fr * rev_i)],
                            axis=-1).reshape(L * G * H, SSM_TILES * 2 * SSM_HALF)
    c_fr = jnp.einsum('ghp,gk->gpkh', c_re, eye).reshape(1, G * P, G * H)
    c_fi = jnp.einsum('ghp,gk->gpkh', c_im, eye).reshape(1, G * P, G * H)
    fwd_r = pr[1:].reshape(L, G * P, 1)
    fwd_i = pi[1:].reshape(L, G * P, 1)
    tile_rows = lambda m: m.reshape(L, SSM_TILES, SSM_HALF, G * H)
    c_end = jnp.concatenate([tile_rows(c_fr * fwd_r - c_fi * fwd_i), -tile_rows(c_fr * fwd_i + c_fi * fwd_r)],
                            axis=2)
    c_end = jnp.transpose(c_end, (1, 2, 0, 3)).reshape(SSM_TILES * 2 * SSM_HALF, L * G * H)
    lam = lambda x: x.reshape(SSM_TILES, 1, SSM_HALF)
    qr, qi = pr[L], pi[L]
    scan_r, scan_i = [], []
    for _ in range(n_levels):
        scan_r.append(lam(qr))
        scan_i.append(lam(qi))
        qr, qi = qr * qr - qi * qi, 2.0 * qr * qi
    return (k_lag.astype(BF16), b_end.astype(BF16), c_end.astype(BF16), jnp.stack(scan_r), jnp.stack(scan_i))


def _cmul(x, cr, ci):
    re, im = x[:, :SSM_HALF], x[:, SSM_HALF:]
    return jnp.concatenate([cr * re - ci * im, cr * im + ci * re], axis=1)


def _s5_kernel(ulo_ref, uhi_ref, klag_ref, bend_ref, cend_ref, scanr_ref, scani_ref, d_ref,
               gluw_ref, glub_ref, olo_ref, ohi_ref, utb_ref, ucm_ref, acc_ref, xin_ref, *, n_chunks, n_levels):
    L, nc, W = SSM_CHUNK, n_chunks, SSM_DIM
    tile_w = 2 * SSM_HALF
    for s in range(L):
        rows = jnp.concatenate([ref[pl.ds(s, nc, stride=L), :] for ref in (ulo_ref, uhi_ref)], axis=1)
        acc_ref[s * nc:(s + 1) * nc, :] = d_ref[...] * rows
        utb_ref[s * nc:(s + 1) * nc, :] = rows.astype(BF16)
        ucm_ref[:, s * W:(s + 1) * W] = rows.astype(BF16)
    for lag in range(L):
        acc_ref[lag * nc:, :] += jnp.dot(utb_ref[0:(L - lag) * nc, :], klag_ref[lag], preferred_element_type=F32)
    e = jnp.dot(ucm_ref[...], bend_ref[...], preferred_element_type=F32)
    row = lax.broadcasted_iota(jnp.int32, (nc, tile_w), 0)
    for j in range(SSM_TILES):
        x = e[:, j * tile_w:(j + 1) * tile_w]
        for k in range(n_levels):
            sh = jnp.where(row >= (1 << k), pltpu.roll(x, 1 << k, 0), 0.0)
            x = x + _cmul(sh, scanr_ref[k, j], scani_ref[k, j])
        xin = jnp.where(row >= 1, pltpu.roll(x, 1, 0), 0.0)
        xin_ref[:, j * tile_w:(j + 1) * tile_w] = xin.astype(BF16)
    ys = jnp.dot(xin_ref[...], cend_ref[...], preferred_element_type=F32)
    for t in range(L):
        y = jax.nn.gelu(acc_ref[t * nc:(t + 1) * nc, :] + ys[:, t * W:(t + 1) * W])
        gate = jnp.dot(y.astype(BF16), gluw_ref[...], preferred_element_type=F32) + glub_ref[...]
        y = y * jax.nn.sigmoid(gate)
        olo_ref[pl.ds(t, nc, stride=L), :] = y[:, :SSM_LANE_TILE]
        ohi_ref[pl.ds(t, nc, stride=L), :] = y[:, SSM_LANE_TILE:]


def _s5_call(su_lo, su_hi, ops, d, glu_w, glu_b):
    bsz, t, half = su_lo.shape
    width = 2 * half
    nc = t // SSM_CHUNK
    n_levels = ops[-1].shape[0]
    consts = list(ops) + [d.reshape(1, width).astype(F32), glu_w, glu_b.reshape(1, width).astype(F32)]
    kern = functools.partial(_s5_kernel, n_chunks=nc, n_levels=n_levels)
    seq = pl.BlockSpec((None, t, half), lambda b: (b, 0, 0))
    return pl.pallas_call(
        kern,
        grid=(bsz,),
        in_specs=[seq, seq] + [_const_spec(x.shape) for x in consts],
        out_specs=[seq, seq],
        out_shape=[jax.ShapeDtypeStruct((bsz, t, half), F32)] * 2,
        scratch_shapes=[pltpu.VMEM((t, width), BF16), pltpu.VMEM((nc, SSM_CHUNK * width), BF16),
                        pltpu.VMEM((t, width), F32), pltpu.VMEM((nc, SSM_TILES * 2 * SSM_HALF), BF16)],
        compiler_params=_params("parallel"),
    )(su_lo, su_hi, *consts)


def kernel(x, rel_bias, ln_pre_ffn1, ln_post_ffn1, ffn1_w_gate, ffn1_w_up, ffn1_w_down, ln_pre_mix, ln_post_mix, w_in, w_out, att_sinks, rwkv_mu, rwkv_w0, rwkv_w2, rwkv_a0, rwkv_a2, rwkv_g2, rwkv_k_k, rwkv_k_a, rwkv_r_k, rwkv_gn_w, rwkv_gn_b, ssm_lambda_re, ssm_lambda_im, ssm_log_dt, ssm_b_re, ssm_b_im, ssm_c_re, ssm_c_im, ssm_d, ssm_glu_w, ssm_glu_b, ln_pre_ffn2, ln_post_ffn2, ffn2_w_gate, ffn2_w_up, ffn2_w_down):
    bsz, t, d = x.shape
    n = bsz * t
    depth = w_in.shape[0]
    tm = 512 if n % 512 == 0 else 128
    ff_chunk = 256
    rwkv_chunks = max(c for c in (4, 2, 1) if t % (c * RWKV_CHUNK) == 0)
    bf = lambda w: w.astype(BF16)
    ffn1 = (ln_pre_ffn1, ln_post_ffn1, bf(ffn1_w_gate), bf(ffn1_w_up), bf(ffn1_w_down))
    ffn2 = (ln_pre_ffn2, ln_post_ffn2, bf(ffn2_w_gate), bf(ffn2_w_up), bf(ffn2_w_down))
    w_in_b, w_out_b, glu_w_b = bf(w_in), bf(w_out), bf(ssm_glu_w)
    bias = _band_bias(rel_bias)
    h = x.reshape(n, d)
    seq = lambda z: z.reshape(bsz, t, z.shape[-1])
    flat = lambda z: z.reshape(n, z.shape[-1])
    n_levels = max(1, math.ceil(math.log2(t // SSM_CHUNK)))
    for l in range(depth):
        h, q, k, v, rw, su_lo, su_hi = _ffn_call(h, l, *ffn1, tm=tm, ff_chunk=ff_chunk, proj=(ln_pre_mix, w_in_b))
        ya = _attn_call(seq(q), seq(k), seq(v), bias, att_sinks[l].astype(F32))
        yr = _rwkv_call(seq(rw), rwkv_mu[l], rwkv_w0[l], rwkv_w2[l], rwkv_a0[l], rwkv_a2[l], rwkv_g2[l],
                        rwkv_k_k[l], rwkv_k_a[l], rwkv_r_k[l], rwkv_gn_w[l], rwkv_gn_b[l], n_chunks=rwkv_chunks)
        ops = _s5_operators(ssm_lambda_re[l], ssm_lambda_im[l], ssm_log_dt[l], ssm_b_re[l], ssm_b_im[l],
                            ssm_c_re[l], ssm_c_im[l], n_levels=n_levels)
        ys_lo, ys_hi = _s5_call(seq(su_lo), seq(su_hi), ops, ssm_d[l], glu_w_b[l], ssm_glu_b[l])
        h = _ffn_call(h, l, *ffn2, tm=tm, ff_chunk=ff_chunk,
                      mix=([flat(ya), flat(yr), flat(ys_lo), flat(ys_hi)], w_out_b, ln_post_mix))
    return h.reshape(bsz, t, d)
```

```python
import functools
import math

import jax
import jax.numpy as jnp
from jax import lax
from jax.experimental import pallas as pl
from jax.experimental.pallas import tpu as pltpu

F32 = jnp.float32
BF16 = jnp.bfloat16

HEAD_DIM = 64
ATT_HEADS = 6
ATT_KV_HEADS = 2
ATT_GROUP = ATT_HEADS // ATT_KV_HEADS
ATT_DIM = ATT_HEADS * HEAD_DIM
ATT_KV_DIM = ATT_KV_HEADS * HEAD_DIM
WINDOW = 128
N_BUCKETS = 32
MAX_DISTANCE = 128
RWKV_HEADS = 6
RWKV_DIM = RWKV_HEADS * HEAD_DIM
DECAY_LORA = 32
ICLR_LORA = 32
GATE_LORA = 64
LORA_COLS = DECAY_LORA + ICLR_LORA + GATE_LORA
RWKV_COLS = 3 * RWKV_DIM + LORA_COLS
GN_EPS = 64e-5
SSM_GROUPS = 16
SSM_GROUP_CH = 16
SSM_DIM = SSM_GROUPS * SSM_GROUP_CH
SSM_STATE = 64
RMS_EPS = 1e-6
MASK_VALUE = -1e30

VMEM_LIMIT_BYTES = 56 * 1024 * 1024
RWKV_CHUNK = 64
SSM_CHUNK = 8


def _params(*sem):
    return pltpu.CompilerParams(dimension_semantics=sem, vmem_limit_bytes=VMEM_LIMIT_BYTES)


def _rms(x, g):
    return x * lax.rsqrt(jnp.mean(x * x, axis=-1, keepdims=True) + RMS_EPS) * g


def _const_spec(shape):
    nd = len(shape)
    return pl.BlockSpec(shape, lambda *_: (0,) * nd, pipeline_mode=pl.Buffered(1))


def _layer_spec(stacked, layer):
    nd = stacked.ndim - 1
    return pl.BlockSpec((None,) + stacked.shape[1:], lambda *_: (layer,) + (0,) * nd, pipeline_mode=pl.Buffered(1))


_IN_SPLITS = ((ATT_DIM, BF16), (ATT_KV_DIM, BF16), (ATT_KV_DIM, BF16), (RWKV_COLS, F32),
              (SSM_DIM // 2, F32), (SSM_DIM // 2, F32))
_N_MIX = 4


def _ffn_kernel(*refs, ff_chunk, n_chunks, mix_in, proj_out):
    it = iter(refs)
    x_ref = next(it)
    if mix_in:
        y_refs = [next(it) for _ in range(_N_MIX)]
        wout_ref, gmix_ref = next(it), next(it)
    gpre_ref, gpost_ref, wg_ref, wu_ref, wd_ref = (next(it) for _ in range(5))
    if proj_out:
        gin_ref, win_ref = next(it), next(it)
    o_ref = next(it)
    if proj_out:
        proj_refs = [next(it) for _ in _IN_SPLITS]
    acc_ref = next(it)

    x = x_ref[...]
    if mix_in:
        y = jnp.concatenate([r[...].astype(BF16) for r in y_refs], axis=-1)
        x = x + _rms(jnp.dot(y, wout_ref[...], preferred_element_type=F32), gmix_ref[...])
    xn = _rms(x, gpre_ref[...]).astype(BF16)
    for c in range(n_chunks):
        sl = slice(c * ff_chunk, (c + 1) * ff_chunk)
        g = jnp.dot(xn, wg_ref[:, sl], preferred_element_type=F32)
        u = jnp.dot(xn, wu_ref[:, sl], preferred_element_type=F32)
        a = (g * jax.nn.sigmoid(g) * u).astype(BF16)
        d = jnp.dot(a, wd_ref[sl, :], preferred_element_type=F32)
        if c == 0:
            acc_ref[...] = d
        else:
            acc_ref[...] += d
    out = x + 0.5 * _rms(acc_ref[...], gpost_ref[...])
    o_ref[...] = out
    if proj_out:
        p = jnp.dot(_rms(out, gin_ref[...]).astype(BF16), win_ref[...], preferred_element_type=F32)
        off = 0
        for ref, (width, dtype) in zip(proj_refs, _IN_SPLITS):
            ref[...] = p[:, off:off + width].astype(dtype)
            off += width


def _ffn_call(h, layer, g_pre, g_post, wg, wu, wd, *, tm, ff_chunk, mix=None, proj=None):
    n, d = h.shape
    dff = wg.shape[-1]
    row = lambda w: pl.BlockSpec((tm, w), lambda i: (i, 0))
    vec = lambda g: g.reshape(g.shape[0], 1, d)
    args, specs = [h], [row(d)]
    if mix is not None:
        ys, w_out, g_mix = mix
        args += list(ys) + [w_out, vec(g_mix)]
        specs += [row(y.shape[-1]) for y in ys]
    args += [vec(g_pre), vec(g_post), wg, wu, wd]
    if proj is not None:
        g_in, w_in = proj
        args += [vec(g_in), w_in]
    specs += [_layer_spec(a, layer) for a in args[len(specs):]]
    out_specs, out_shape = [row(d)], [jax.ShapeDtypeStruct((n, d), F32)]
    if proj is not None:
        out_specs += [row(w) for w, _ in _IN_SPLITS]
        out_shape += [jax.ShapeDtypeStruct((n, w), dt) for w, dt in _IN_SPLITS]
    kern = functools.partial(_ffn_kernel, ff_chunk=ff_chunk, n_chunks=dff // ff_chunk,
                             mix_in=mix is not None, proj_out=proj is not None)
    outs = pl.pallas_call(
        kern,
        grid=(n // tm,),
        in_specs=specs,
        out_specs=out_specs,
        out_shape=out_shape,
        scratch_shapes=[pltpu.VMEM((tm, d), F32)],
        compiler_params=_params("parallel"),
    )(*args)
    return outs if proj is not None else outs[0]


ATT_BLOCKS = 2


def _band_bias(rel_bias):
    qi = jnp.arange(WINDOW)[:, None]
    kj = jnp.arange(2 * WINDOW)[None, :]
    rel = qi + WINDOW - kj
    in_window = (rel >= 0) & (rel < WINDOW)
    n = jnp.maximum(rel, 0)
    max_exact = N_BUCKETS // 2
    nf = jnp.maximum(n, 1).astype(F32)
    large = max_exact + (jnp.log(nf / max_exact) / math.log(MAX_DISTANCE / max_exact)
                         * (N_BUCKETS - max_exact)).astype(jnp.int32)
    large = jnp.minimum(large, N_BUCKETS - 1)
    bucket = jnp.where(n < max_exact, n, large)
    onehot = (bucket[..., None] == jnp.arange(N_BUCKETS)).astype(F32)
    bias = jnp.einsum('qkb,bh->hqk', onehot, rel_bias.astype(F32), precision=lax.Precision.HIGHEST)
    return jnp.transpose(jnp.where(in_window[None], bias, MASK_VALUE), (0, 2, 1))


def _attn_body(first, q, k_prev, k_cur, v_prev, v_cur, bias, sink):
    q = q * (HEAD_DIM ** -0.5)
    k = jnp.concatenate([k_prev, k_cur], axis=0)
    v = jnp.concatenate([v_prev, v_cur], axis=0)
    row = lax.broadcasted_iota(jnp.int32, (1, 2 * WINDOW, 1), 1)
    has_prev = jnp.logical_or(jnp.logical_not(first), row >= WINDOW)
    kv_of = [h // ATT_GROUP for h in range(ATT_HEADS)]
    head = lambda x, j: x[:, j * HEAD_DIM:(j + 1) * HEAD_DIM]
    blocks = []
    for i in range(ATT_BLOCKS):
        band = slice(i * WINDOW, (i + 2) * WINDOW)
        qi = q[i * WINDOW:(i + 1) * WINDOW]
        v_t = v[band].astype(F32).T.astype(BF16)
        qs = jnp.stack([head(qi, h) for h in range(ATT_HEADS)], axis=0)
        ks = jnp.stack([head(k[band], j) for j in kv_of], axis=0)
        vs = jnp.stack([v_t[j * HEAD_DIM:(j + 1) * HEAD_DIM] for j in kv_of], axis=0)
        s = lax.dot_general(ks, qs, (((2,), (2,)), ((0,), (0,))), preferred_element_type=F32)
        s = s + bias
        if i == 0:
            s = jnp.where(has_prev, s, MASK_VALUE)
        m = jnp.maximum(jnp.max(s, axis=1, keepdims=True), sink)
        p = jnp.exp(s - m)
        denom = jnp.sum(p, axis=1, keepdims=True) + jnp.exp(sink - m)
        o = lax.dot_general(vs, p.astype(BF16), (((2,), (1,)), ((0,), (0,))), preferred_element_type=F32)
        o = o / denom
        blocks.append(o.reshape(ATT_DIM, WINDOW).T)
    return jnp.concatenate(blocks, axis=0)


RWKV_PAIR = 2
RWKV_GROUP_HEADS = 4
RWKV_GROUP_W = RWKV_GROUP_HEADS * HEAD_DIM
RWKV_GROUPS = RWKV_PAIR * RWKV_HEADS // RWKV_GROUP_HEADS


def _gdot(a, b, contract):
    return lax.dot_general(a.astype(BF16), b.astype(BF16), (contract, ((0,), (0,))),
                           preferred_element_type=F32)


def _gmm(a, b):
    return _gdot(a, b, ((2,), (1,)))


def _gmm_nt(a, b):
    return _gdot(a, b, ((2,), (2,)))


def _gmm_tn(a, b):
    return _gdot(a, b, ((1,), (1,)))


def _flat_groups(x):
    return jnp.stack([x[:, j * RWKV_GROUP_W:(j + 1) * RWKV_GROUP_W] for j in range(RWKV_GROUPS)], axis=0)


def _head_rows(x):
    lane_head = lax.broadcasted_iota(jnp.int32, (1, RWKV_GROUP_W), 1) // HEAD_DIM
    zero = jnp.zeros((), x.dtype)
    groups = []
    for j in range(RWKV_GROUPS):
        xg = x[:, j * RWKV_GROUP_W:(j + 1) * RWKV_GROUP_W]
        groups.append(jnp.concatenate([jnp.where(lane_head == i, xg, zero) for i in range(RWKV_GROUP_HEADS)], axis=0))
    return jnp.stack(groups, axis=0)


def _rwkv_kernel(p_ref, mu_ref, w0_ref, a0_ref, kk_ref, ka_ref, rk_ref, gnw_ref, gnb_ref,
                 lora_ref, hb_ref, tril_ref, o_ref, prev_ref, s_ref, *, n_chunks):
    C, D, W = RWKV_CHUNK, RWKV_DIM, RWKV_GROUP_W
    tt = n_chunks * C

    @pl.when(pl.program_id(1) == 0)
    def _():
        prev_ref[...] = jnp.zeros_like(prev_ref)
        s_ref[...] = jnp.zeros_like(s_ref)

    p = p_ref[...].reshape(RWKV_PAIR * tt, RWKV_COLS)
    row = lax.broadcasted_iota(jnp.int32, p.shape, 0)
    prev = pltpu.roll(p, 1, 0)
    for i in range(RWKV_PAIR):
        prev = jnp.where(row == i * tt, prev_ref[i], prev)
        prev_ref[i] = p[(i + 1) * tt - 1:(i + 1) * tt, :]
    xs = p + (prev - p) * mu_ref[...]
    lo = xs[:, 3 * D:3 * D + LORA_COLS]
    lane = lax.broadcasted_iota(jnp.int32, lo.shape, 1)
    act = jnp.where(lane < DECAY_LORA, jnp.tanh(lo),
                    jnp.where(lane < DECAY_LORA + ICLR_LORA, lo, jax.nn.sigmoid(lo)))
    proj = jnp.dot(act.astype(BF16), lora_ref[...], preferred_element_type=F32)

    def pair(x):
        return jnp.concatenate([x[i * tt:(i + 1) * tt] for i in range(RWKV_PAIR)], axis=1)

    def head_sum(x):
        xb = x.astype(BF16)
        return jnp.concatenate([jnp.dot(xb[:, j * W:(j + 1) * W], hb_ref[...], preferred_element_type=F32)
                                for j in range(RWKV_GROUPS)], axis=1)

    r, k, v = pair(xs[:, 0:D]), pair(xs[:, D:2 * D]), pair(xs[:, 2 * D:3 * D])
    z = -(w0_ref[...] + pair(proj[:, 0:D]))
    softplus = jnp.maximum(z, 0.0) + jnp.log(1.0 + jnp.exp(-jnp.abs(z)))
    logw = -jnp.exp(-softplus - 0.5)
    a = jax.nn.sigmoid(a0_ref[...] + pair(proj[:, D:2 * D]))
    g = pair(proj[:, 2 * D:3 * D])
    kk = k * kk_ref[...]
    kk = kk / jnp.maximum(jnp.sqrt(head_sum(kk * kk)), 1e-12)
    k = k * (1.0 + (a - 1.0) * ka_ref[...])

    logw_hi = logw.astype(BF16)
    logw_lo = (logw - logw_hi.astype(F32)).astype(BF16)
    cum = (jnp.dot(tril_ref[...], logw_hi, preferred_element_type=F32)
           + jnp.dot(tril_ref[...], logw_lo, preferred_element_type=F32))
    e_pos = jnp.exp(cum)
    e_neg = jnp.exp(-cum)
    kb = kk * a
    r_t = (r * e_pos).astype(BF16)
    k_t = (k * e_neg).astype(BF16)
    b_t = (kb * e_neg).astype(BF16)
    a_t = (-kk * jnp.exp(cum - logw)).astype(BF16)
    v_b = v.astype(BF16)

    r64 = lax.broadcasted_iota(jnp.int32, (RWKV_GROUP_HEADS * C, W), 0) % C
    c64 = lax.broadcasted_iota(jnp.int32, (RWKV_GROUP_HEADS * C, W), 1) % C
    strict = (r64 > c64)[None]
    eye = (lax.broadcasted_iota(jnp.int32, (RWKV_GROUP_HEADS * C, W), 0)
           == lax.broadcasted_iota(jnp.int32, (RWKV_GROUP_HEADS * C, W), 1))[None].astype(F32)
    incl = (lax.broadcasted_iota(jnp.int32, (C, W), 0) >= lax.broadcasted_iota(jnp.int32, (C, W), 1) % C)[None]

    def pair_lanes(x, c):
        return x[c * C:(c + 1) * C]

    ys = []
    for c in range(n_chunks):
        g_end = pair_lanes(e_pos, c)[C - 1:C, :]
        k_end = (pair_lanes(k * e_neg, c) * g_end).astype(BF16)
        b_end = (pair_lanes(kb * e_neg, c) * g_end).astype(BF16)
        ax, bx, kx = _head_rows(pair_lanes(a_t, c)), _head_rows(pair_lanes(b_t, c)), _head_rows(pair_lanes(k_t, c))
        vx = _head_rows(pair_lanes(v_b, c))
        rf = _flat_groups(pair_lanes(r_t, c))
        sc = _gmm_nt(jnp.concatenate([ax, rf], axis=1), jnp.concatenate([bx, kx], axis=1))
        a_ab = jnp.where(strict, sc[:, :4 * C, :W], 0.0)
        a_ak = jnp.where(strict, sc[:, :4 * C, W:], 0.0)
        a_rb = jnp.where(incl, sc[:, 4 * C:, :W], 0.0)
        a_rk = jnp.where(incl, sc[:, 4 * C:, W:], 0.0)
        inv = eye + a_ab
        powr = _gmm(a_ab, a_ab)
        span = 2
        while span < C:
            if 2 * span < C:
                both = _gmm(powr, jnp.concatenate([inv.astype(BF16), powr.astype(BF16)], axis=-1))
                inv, powr = inv + both[..., :W], both[..., W:]
            else:
                inv = inv + _gmm(powr, inv)
            span *= 2
        akv = _gmm(a_ak, vx)

        s = s_ref[...]
        sr = _gmm_nt(jnp.concatenate([ax, rf], axis=1), s)
        u = _gmm(inv, sr[:, :4 * C] + akv)
        uv = jnp.concatenate([u.astype(BF16), vx], axis=1)
        y = sr[:, 4 * C:] + _gmm(jnp.concatenate([a_rb, a_rk], axis=-1), uv)
        s_ref[...] = (s * _flat_groups(g_end)
                      + _gmm_tn(uv, jnp.concatenate([_head_rows(b_end), _head_rows(k_end)], axis=1)))
        ys.append(jnp.concatenate([y[j] for j in range(RWKV_GROUPS)], axis=-1))

    y = jnp.concatenate(ys, axis=0)
    inv_n = 1.0 / HEAD_DIM
    yc = y - head_sum(y) * inv_n
    var = head_sum(yc * yc) * inv_n
    y = yc * lax.rsqrt(var + GN_EPS) * gnw_ref[...] + gnb_ref[...]
    out = (y + head_sum(r * k * rk_ref[...]) * v) * g
    for i in range(RWKV_PAIR):
        o_ref[i] = out[:, i * D:(i + 1) * D].astype(o_ref.dtype)


def _mix_kernel(sink_ref, bias_ref, q_ref, kp_ref, kc_ref, vp_ref, vc_ref, *rest, n_chunks):
    rwkv_in, (ya_ref, yr_ref, prev_ref, s_ref) = rest[:-4], rest[-4:]
    first = pl.program_id(1) == 0
    for i in range(RWKV_PAIR):
        ya_ref[i] = _attn_body(first, q_ref[i], kp_ref[i], kc_ref[i], vp_ref[i], vc_ref[i], bias_ref[...],
                               sink_ref[...]).astype(ya_ref.dtype)
    _rwkv_kernel(*rwkv_in, yr_ref, prev_ref, s_ref, n_chunks=n_chunks)


def _mix_call(q, k, v, bias, sinks, rw, mu, w0, w2, a0, a2, g2, k_k, k_a, r_k, gn_w, gn_b):
    bsz, t, cols = rw.shape
    C, D = RWKV_CHUNK, RWKV_DIM
    tt = ATT_BLOCKS * WINDOW
    n_chunks = tt // C
    lora = jnp.zeros((LORA_COLS, 3 * D), F32)
    lora = lora.at[0:DECAY_LORA, 0:D].set(w2)
    lora = lora.at[DECAY_LORA:DECAY_LORA + ICLR_LORA, D:2 * D].set(a2)
    lora = lora.at[DECAY_LORA + ICLR_LORA:, 2 * D:].set(g2)
    head_id = jnp.arange(RWKV_GROUP_W) // HEAD_DIM
    hb = (head_id[:, None] == head_id[None, :]).astype(BF16)
    pos = jnp.arange(tt)
    tril = ((pos[:, None] >= pos[None, :]) & (pos[:, None] // C == pos[None, :] // C)).astype(BF16)
    vec = lambda x: x.reshape(1, -1).astype(F32)
    both = lambda x: jnp.tile(vec(x), (1, RWKV_PAIR))
    vecs = [vec(mu), both(w0), both(a0), both(k_k), both(k_a), both(r_k), both(gn_w), both(gn_b)]
    consts = vecs + [lora.astype(BF16), hb, tril]
    cur = lambda b, c: (b, c, 0)
    prv = lambda b, c: (b, jnp.maximum(c * ATT_BLOCKS - 1, 0), 0)
    cur_kv = pl.BlockSpec((RWKV_PAIR, tt, ATT_KV_DIM), cur)
    prv_kv = pl.BlockSpec((RWKV_PAIR, WINDOW, ATT_KV_DIM), prv)
    head_blk = pl.BlockSpec((RWKV_PAIR, tt, D), cur)
    sinks = sinks.reshape(ATT_HEADS, 1, 1).astype(F32)
    return pl.pallas_call(
        functools.partial(_mix_kernel, n_chunks=n_chunks),
        grid=(bsz // RWKV_PAIR, t // tt),
        in_specs=[_const_spec(sinks.shape), _const_spec(bias.shape),
                  pl.BlockSpec((RWKV_PAIR, tt, ATT_DIM), cur), prv_kv, cur_kv, prv_kv, cur_kv,
                  pl.BlockSpec((RWKV_PAIR, tt, cols), cur)] + [_const_spec(x.shape) for x in consts],
        out_specs=[head_blk, head_blk],
        out_shape=[jax.ShapeDtypeStruct((bsz, t, D), BF16)] * 2,
        scratch_shapes=[pltpu.VMEM((RWKV_PAIR, 1, cols), F32),
                        pltpu.VMEM((RWKV_GROUPS, RWKV_GROUP_W, RWKV_GROUP_W), F32)],
        compiler_params=_params("parallel", "arbitrary"),
    )(sinks, bias, q, k, k, v, v, rw, *consts)


SSM_TILE_GROUPS = 2
SSM_TILES = SSM_GROUPS // SSM_TILE_GROUPS
SSM_HALF = SSM_TILE_GROUPS * SSM_STATE
SSM_LANE_TILE = SSM_DIM // 2


def _s5_operators(lam_re, lam_im, log_dt, b_re, b_im, c_re, c_im, n_levels):
    L, G, P, H = SSM_CHUNK, SSM_GROUPS, SSM_STATE, SSM_GROUP_CH
    hi = lax.Precision.HIGHEST
    lr = jnp.minimum(lam_re.astype(F32), -1e-4)
    li = lam_im.astype(F32)
    dt = jnp.exp(log_dt.astype(F32))[:, None]
    n = jnp.arange(L + 1, dtype=F32)[:, None, None]
    mag = jnp.exp(lr * dt * n)
    pr = mag * jnp.cos(li * dt * n)
    pi = mag * jnp.sin(li * dt * n)
    den = lr * lr + li * li
    nr, ni = pr[1] - 1.0, pi[1]
    fr = (nr * lr + ni * li) / den
    fi = (ni * lr - nr * li) / den
    b_re, b_im = b_re.astype(F32), b_im.astype(F32)
    bb_re = fr[..., None] * b_re - fi[..., None] * b_im
    bb_im = fr[..., None] * b_im + fi[..., None] * b_re
    c_re, c_im = c_re.astype(F32), c_im.astype(F32)
    cl_re = c_re[None] * pr[:L, :, None, :] - c_im[None] * pi[:L, :, None, :]
    cl_im = c_re[None] * pi[:L, :, None, :] + c_im[None] * pr[:L, :, None, :]
    kern = (jnp.einsum('nghp,gpk->nghk', cl_re, bb_re, precision=hi)
            - jnp.einsum('nghp,gpk->nghk', cl_im, bb_im, precision=hi))
    eye = jnp.eye(G, dtype=F32)
    k_lag = jnp.einsum('lgab,gk->lgbka', kern, eye).reshape(L, G * H, G * H)

    flat = lambda x: x.reshape(x.shape[0], 1, G * P)
    b_fr = jnp.einsum('gph,gk->ghkp', bb_re, eye).reshape(1, G * H, G * P)
    b_fi = jnp.einsum('gph,gk->ghkp', bb_im, eye).reshape(1, G * H, G * P)
    rev_r, rev_i = flat(pr[:L][::-1]), flat(pi[:L][::-1])
    tile_cols = lambda m: m.reshape(L, G * H, SSM_TILES, SSM_HALF)
    b_end = jnp.concatenate([tile_cols(b_fr * rev_r - b_fi * rev_i), tile_cols(b_fi * rev_r + b_fr * rev_i)],
                            axis=-1).reshape(L * G * H, SSM_TILES * 2 * SSM_HALF)
    c_fr = jnp.einsum('ghp,gk->gpkh', c_re, eye).reshape(1, G * P, G * H)
    c_fi = jnp.einsum('ghp,gk->gpkh', c_im, eye).reshape(1, G * P, G * H)
    fwd_r = pr[1:].reshape(L, G * P, 1)
    fwd_i = pi[1:].reshape(L, G * P, 1)
    tile_rows = lambda m: m.reshape(L, SSM_TILES, SSM_HALF, G * H)
    c_end = jnp.concatenate([tile_rows(c_fr * fwd_r - c_fi * fwd_i), -tile_rows(c_fr * fwd_i + c_fi * fwd_r)],
                            axis=2)
    c_end = jnp.transpose(c_end, (1, 2, 0, 3)).reshape(SSM_TILES * 2 * SSM_HALF, L * G * H)
    lam = lambda x: x.reshape(SSM_TILES, 1, SSM_HALF)
    qr, qi = pr[L], pi[L]
    scan_r, scan_i = [], []
    for _ in range(n_levels):
        scan_r.append(lam(qr))
        scan_i.append(lam(qi))
        qr, qi = qr * qr - qi * qi, 2.0 * qr * qi
    lt, st = SSM_LANE_TILE, SSM_TILES * SSM_HALF
    b_end = b_end.reshape(L, G * H, 2 * st)
    b_end = jnp.stack([b_end[:, i * lt:(i + 1) * lt, i * st:(i + 1) * st].reshape(L * lt, st) for i in range(2)])
    c_end = c_end.reshape(2 * st, L, G * H)
    c_end = jnp.stack([c_end[i * st:(i + 1) * st, :, i * lt:(i + 1) * lt].reshape(st, L * lt) for i in range(2)])
    return (k_lag.astype(BF16), b_end.astype(BF16), c_end.astype(BF16), jnp.stack(scan_r), jnp.stack(scan_i))


def _cmul(x, cr, ci):
    re, im = x[:, :SSM_HALF], x[:, SSM_HALF:]
    return jnp.concatenate([cr * re - ci * im, cr * im + ci * re], axis=1)


def _s5_kernel(ulo_ref, uhi_ref, klag_ref, bend_ref, cend_ref, scanr_ref, scani_ref, d_ref,
               gluw_ref, glub_ref, olo_ref, ohi_ref, utb_ref, ucm_ref, acc_ref, xin_ref, *, n_chunks, n_levels):
    L, nc, W, lt = SSM_CHUNK, n_chunks, SSM_DIM, SSM_LANE_TILE
    tile_w = 2 * SSM_HALF
    for s in range(L):
        halves = [ref[pl.ds(s, nc, stride=L), :] for ref in (ulo_ref, uhi_ref)]
        rows = jnp.concatenate(halves, axis=1)
        acc_ref[s * nc:(s + 1) * nc, :] = d_ref[...] * rows
        utb_ref[s * nc:(s + 1) * nc, :] = rows.astype(BF16)
        for i in range(2):
            ucm_ref[i, :, s * lt:(s + 1) * lt] = halves[i].astype(BF16)
    for lag in range(L):
        acc_ref[lag * nc:, :] += jnp.dot(utb_ref[0:(L - lag) * nc, :], klag_ref[lag], preferred_element_type=F32)
    e = jnp.concatenate([jnp.dot(ucm_ref[i], bend_ref[i], preferred_element_type=F32) for i in range(2)], axis=1)
    row = lax.broadcasted_iota(jnp.int32, (nc, tile_w), 0)
    for j in range(SSM_TILES):
        x = e[:, j * tile_w:(j + 1) * tile_w]
        for k in range(n_levels):
            sh = jnp.where(row >= (1 << k), pltpu.roll(x, 1 << k, 0), 0.0)
            x = x + _cmul(sh, scanr_ref[k, j], scani_ref[k, j])
        xin = jnp.where(row >= 1, pltpu.roll(x, 1, 0), 0.0)
        xin_ref[:, j * tile_w:(j + 1) * tile_w] = xin.astype(BF16)
    half = SSM_TILES * SSM_HALF
    ys = [jnp.dot(xin_ref[:, i * half:(i + 1) * half], cend_ref[i], preferred_element_type=F32)
          for i in range(2)]
    for t in range(L):
        y_state = jnp.concatenate([y[:, t * lt:(t + 1) * lt] for y in ys], axis=1)
        y = jax.nn.gelu(acc_ref[t * nc:(t + 1) * nc, :] + y_state)
        gate = jnp.dot(y.astype(BF16), gluw_ref[...], preferred_element_type=F32) + glub_ref[...]
        y = y * jax.nn.sigmoid(gate)
        olo_ref[pl.ds(t, nc, stride=L), :] = y[:, :lt]
        ohi_ref[pl.ds(t, nc, stride=L), :] = y[:, lt:]


def _s5_call(su_lo, su_hi, ops, d, glu_w, glu_b):
    bsz, t, half = su_lo.shape
    width = 2 * half
    nc = t // SSM_CHUNK
    n_levels = ops[-1].shape[0]
    consts = list(ops) + [d.reshape(1, width).astype(F32), glu_w, glu_b.reshape(1, width).astype(F32)]
    kern = functools.partial(_s5_kernel, n_chunks=nc, n_levels=n_levels)
    seq = pl.BlockSpec((None, t, half), lambda b: (b, 0, 0))
    return pl.pallas_call(
        kern,
        grid=(bsz,),
        in_specs=[seq, seq] + [_const_spec(x.shape) for x in consts],
        out_specs=[seq, seq],
        out_shape=[jax.ShapeDtypeStruct((bsz, t, half), F32)] * 2,
        scratch_shapes=[pltpu.VMEM((t, width), BF16), pltpu.VMEM((2, nc, SSM_CHUNK * half), BF16),
                        pltpu.VMEM((t, width), F32), pltpu.VMEM((nc, SSM_TILES * 2 * SSM_HALF), BF16)],
        compiler_params=_params("parallel"),
    )(su_lo, su_hi, *consts)


def kernel(x, rel_bias, ln_pre_ffn1, ln_post_ffn1, ffn1_w_gate, ffn1_w_up, ffn1_w_down, ln_pre_mix, ln_post_mix, w_in, w_out, att_sinks, rwkv_mu, rwkv_w0, rwkv_w2, rwkv_a0, rwkv_a2, rwkv_g2, rwkv_k_k, rwkv_k_a, rwkv_r_k, rwkv_gn_w, rwkv_gn_b, ssm_lambda_re, ssm_lambda_im, ssm_log_dt, ssm_b_re, ssm_b_im, ssm_c_re, ssm_c_im, ssm_d, ssm_glu_w, ssm_glu_b, ln_pre_ffn2, ln_post_ffn2, ffn2_w_gate, ffn2_w_up, ffn2_w_down):
    bsz, t, d = x.shape
    n = bsz * t
    depth = w_in.shape[0]
    tm = 512 if n % 512 == 0 else 128
    ff_chunk = 256
    bf = lambda w: w.astype(BF16)
    ffn1 = (ln_pre_ffn1, ln_post_ffn1, bf(ffn1_w_gate), bf(ffn1_w_up), bf(ffn1_w_down))
    ffn2 = (ln_pre_ffn2, ln_post_ffn2, bf(ffn2_w_gate), bf(ffn2_w_up), bf(ffn2_w_down))
    w_in_b, w_out_b, glu_w_b = bf(w_in), bf(w_out), bf(ssm_glu_w)
    bias = _band_bias(rel_bias)
    h = x.reshape(n, d)
    seq = lambda z: z.reshape(bsz, t, z.shape[-1])
    flat = lambda z: z.reshape(n, z.shape[-1])
    n_levels = max(1, math.ceil(math.log2(t // SSM_CHUNK)))
    for l in range(depth):
        h, q, k, v, rw, su_lo, su_hi = _ffn_call(h, l, *ffn1, tm=tm, ff_chunk=ff_chunk, proj=(ln_pre_mix, w_in_b))
        ya, yr = _mix_call(seq(q), seq(k), seq(v), bias, att_sinks[l], seq(rw), rwkv_mu[l], rwkv_w0[l], rwkv_w2[l],
                           rwkv_a0[l], rwkv_a2[l], rwkv_g2[l], rwkv_k_k[l], rwkv_k_a[l], rwkv_r_k[l],
                           rwkv_gn_w[l], rwkv_gn_b[l])
        ops = _s5_operators(ssm_lambda_re[l], ssm_lambda_im[l], ssm_log_dt[l], ssm_b_re[l], ssm_b_im[l],
                            ssm_c_re[l], ssm_c_im[l], n_levels=n_levels)
        ys_lo, ys_hi = _s5_call(seq(su_lo), seq(su_hi), ops, ssm_d[l], glu_w_b[l], ssm_glu_b[l])
        h = _ffn_call(h, l, *ffn2, tm=tm, ff_chunk=ff_chunk,
                      mix=([flat(ya), flat(yr), flat(ys_lo), flat(ys_hi)], w_out_b, ln_post_mix))
    return h.reshape(bsz, t, d)
```

```python
import functools
import math

import jax
import jax.numpy as jnp
from jax import lax
from jax.experimental import pallas as pl
from jax.experimental.pallas import tpu as pltpu

F32 = jnp.float32
BF16 = jnp.bfloat16

HEAD_DIM = 64
ATT_HEADS = 6
ATT_KV_HEADS = 2
ATT_GROUP = ATT_HEADS // ATT_KV_HEADS
ATT_DIM = ATT_HEADS * HEAD_DIM
ATT_KV_DIM = ATT_KV_HEADS * HEAD_DIM
WINDOW = 128
N_BUCKETS = 32
MAX_DISTANCE = 128
RWKV_HEADS = 6
RWKV_DIM = RWKV_HEADS * HEAD_DIM
DECAY_LORA = 32
ICLR_LORA = 32
GATE_LORA = 64
LORA_COLS = DECAY_LORA + ICLR_LORA + GATE_LORA
RWKV_COLS = 3 * RWKV_DIM + LORA_COLS
GN_EPS = 64e-5
SSM_GROUPS = 16
SSM_GROUP_CH = 16
SSM_DIM = SSM_GROUPS * SSM_GROUP_CH
SSM_STATE = 64
RMS_EPS = 1e-6
MASK_VALUE = -1e30

VMEM_LIMIT_BYTES = 56 * 1024 * 1024
RWKV_CHUNK = 64
SSM_CHUNK = 8


def _params(*sem):
    return pltpu.CompilerParams(dimension_semantics=sem, vmem_limit_bytes=VMEM_LIMIT_BYTES)


def _rms(x, g):
    return x * lax.rsqrt(jnp.mean(x * x, axis=-1, keepdims=True) + RMS_EPS) * g


def _const_spec(shape):
    nd = len(shape)
    return pl.BlockSpec(shape, lambda *_: (0,) * nd, pipeline_mode=pl.Buffered(1))


def _layer_spec(stacked, layer):
    nd = stacked.ndim - 1
    return pl.BlockSpec((None,) + stacked.shape[1:], lambda *_: (layer,) + (0,) * nd, pipeline_mode=pl.Buffered(1))


_IN_SPLITS = ((ATT_DIM, BF16), (ATT_KV_DIM, BF16), (ATT_KV_DIM, BF16), (RWKV_COLS, F32),
              (SSM_DIM // 2, F32), (SSM_DIM // 2, F32))
_N_MIX = 4


def _ffn_kernel(*refs, ff_chunk, n_chunks, mix_in, proj_out):
    it = iter(refs)
    x_ref = next(it)
    if mix_in:
        y_refs = [next(it) for _ in range(_N_MIX)]
        wout_ref, gmix_ref = next(it), next(it)
    gpre_ref, gpost_ref, wg_ref, wu_ref, wd_ref = (next(it) for _ in range(5))
    if proj_out:
        gin_ref, win_ref = next(it), next(it)
    o_ref = next(it)
    if proj_out:
        proj_refs = [next(it) for _ in _IN_SPLITS]
    acc_ref = next(it)

    x = x_ref[...]
    if mix_in:
        y = jnp.concatenate([r[...].astype(BF16) for r in y_refs], axis=-1)
        x = x + _rms(jnp.dot(y, wout_ref[...], preferred_element_type=F32), gmix_ref[...])
    xn = _rms(x, gpre_ref[...]).astype(BF16)
    for c in range(n_chunks):
        sl = slice(c * ff_chunk, (c + 1) * ff_chunk)
        g = jnp.dot(xn, wg_ref[:, sl], preferred_element_type=F32)
        u = jnp.dot(xn, wu_ref[:, sl], preferred_element_type=F32)
        a = (g * jax.nn.sigmoid(g) * u).astype(BF16)
        d = jnp.dot(a, wd_ref[sl, :], preferred_element_type=F32)
        if c == 0:
            acc_ref[...] = d
        else:
            acc_ref[...] += d
    out = x + 0.5 * _rms(acc_ref[...], gpost_ref[...])
    o_ref[...] = out
    if proj_out:
        p = jnp.dot(_rms(out, gin_ref[...]).astype(BF16), win_ref[...], preferred_element_type=F32)
        off = 0
        for ref, (width, dtype) in zip(proj_refs, _IN_SPLITS):
            ref[...] = p[:, off:off + width].astype(dtype)
            off += width


def _ffn_call(h, layer, g_pre, g_post, wg, wu, wd, *, tm, ff_chunk, mix=None, proj=None):
    n, d = h.shape
    dff = wg.shape[-1]
    row = lambda w: pl.BlockSpec((tm, w), lambda i: (i, 0))
    vec = lambda g: g.reshape(g.shape[0], 1, d)
    args, specs = [h], [row(d)]
    if mix is not None:
        ys, w_out, g_mix = mix
        args += list(ys) + [w_out, vec(g_mix)]
        specs += [row(y.shape[-1]) for y in ys]
    args += [vec(g_pre), vec(g_post), wg, wu, wd]
    if proj is not None:
        g_in, w_in = proj
        args += [vec(g_in), w_in]
    specs += [_layer_spec(a, layer) for a in args[len(specs):]]
    out_specs, out_shape = [row(d)], [jax.ShapeDtypeStruct((n, d), F32)]
    if proj is not None:
        out_specs += [row(w) for w, _ in _IN_SPLITS]
        out_shape += [jax.ShapeDtypeStruct((n, w), dt) for w, dt in _IN_SPLITS]
    kern = functools.partial(_ffn_kernel, ff_chunk=ff_chunk, n_chunks=dff // ff_chunk,
                             mix_in=mix is not None, proj_out=proj is not None)
    outs = pl.pallas_call(
        kern,
        grid=(n // tm,),
        in_specs=specs,
        out_specs=out_specs,
        out_shape=out_shape,
        scratch_shapes=[pltpu.VMEM((tm, d), F32)],
        compiler_params=_params("parallel"),
    )(*args)
    return outs if proj is not None else outs[0]


ATT_BLOCKS = 2


def _band_bias(rel_bias):
    qi = jnp.arange(WINDOW)[:, None]
    kj = jnp.arange(2 * WINDOW)[None, :]
    rel = qi + WINDOW - kj
    in_window = (rel >= 0) & (rel < WINDOW)
    n = jnp.maximum(rel, 0)
    max_exact = N_BUCKETS // 2
    nf = jnp.maximum(n, 1).astype(F32)
    large = max_exact + (jnp.log(nf / max_exact) / math.log(MAX_DISTANCE / max_exact)
                         * (N_BUCKETS - max_exact)).astype(jnp.int32)
    large = jnp.minimum(large, N_BUCKETS - 1)
    bucket = jnp.where(n < max_exact, n, large)
    onehot = (bucket[..., None] == jnp.arange(N_BUCKETS)).astype(F32)
    bias = jnp.einsum('qkb,bh->hqk', onehot, rel_bias.astype(F32), precision=lax.Precision.HIGHEST)
    return jnp.transpose(jnp.where(in_window[None], bias, MASK_VALUE), (0, 2, 1))


def _attn_body(first, qs_in, k_prevs, k_curs, v_prevs, v_curs, bias, sink):
    row = lax.broadcasted_iota(jnp.int32, (1, 2 * WINDOW, 1), 1)
    has_prev = jnp.logical_or(jnp.logical_not(first), row >= WINDOW)
    kv_of = [h // ATT_GROUP for h in range(ATT_HEADS)]
    head = lambda x, j: x[:, j * HEAD_DIM:(j + 1) * HEAD_DIM]
    qs, ks, vs, masks = [], [], [], []
    for q, k_prev, k_cur, v_prev, v_cur in zip(qs_in, k_prevs, k_curs, v_prevs, v_curs):
        q = q * (HEAD_DIM ** -0.5)
        k = jnp.concatenate([k_prev, k_cur], axis=0)
        v = jnp.concatenate([v_prev, v_cur], axis=0)
        for i in range(ATT_BLOCKS):
            band = slice(i * WINDOW, (i + 2) * WINDOW)
            qi = q[i * WINDOW:(i + 1) * WINDOW]
            v_t = v[band].astype(F32).T.astype(BF16)
            qs += [head(qi, h) for h in range(ATT_HEADS)]
            ks += [head(k[band], j) for j in kv_of]
            vs += [v_t[j * HEAD_DIM:(j + 1) * HEAD_DIM] for j in kv_of]
            keep = has_prev if i == 0 else jnp.ones_like(has_prev)
            masks.append(jnp.broadcast_to(keep, (ATT_HEADS, 2 * WINDOW, 1)))
    n_rep = len(qs) // ATT_HEADS
    qs, ks, vs = jnp.stack(qs, axis=0), jnp.stack(ks, axis=0), jnp.stack(vs, axis=0)
    s = lax.dot_general(ks, qs, (((2,), (2,)), ((0,), (0,))), preferred_element_type=F32)
    s = jnp.where(jnp.concatenate(masks, axis=0), s + jnp.concatenate([bias] * n_rep, axis=0), MASK_VALUE)
    sink = jnp.concatenate([sink] * n_rep, axis=0)
    m = jnp.maximum(jnp.max(s, axis=1, keepdims=True), sink)
    p = jnp.exp(s - m)
    denom = jnp.sum(p, axis=1, keepdims=True) + jnp.exp(sink - m)
    o = lax.dot_general(vs, p.astype(BF16), (((2,), (1,)), ((0,), (0,))), preferred_element_type=F32)
    o = o / denom
    outs = []
    for si in range(len(qs_in)):
        blocks = [o[(si * ATT_BLOCKS + i) * ATT_HEADS:(si * ATT_BLOCKS + i + 1) * ATT_HEADS]
                  .reshape(ATT_DIM, WINDOW).T for i in range(ATT_BLOCKS)]
        outs.append(jnp.concatenate(blocks, axis=0))
    return outs


RWKV_PAIR = 2
RWKV_GROUP_HEADS = 4
RWKV_GROUP_W = RWKV_GROUP_HEADS * HEAD_DIM
RWKV_GROUPS = RWKV_PAIR * RWKV_HEADS // RWKV_GROUP_HEADS


def _gdot(a, b, contract):
    return lax.dot_general(a.astype(BF16), b.astype(BF16), (contract, ((0,), (0,))),
                           preferred_element_type=F32)


def _gmm(a, b):
    return _gdot(a, b, ((2,), (1,)))


def _gmm_nt(a, b):
    return _gdot(a, b, ((2,), (2,)))


def _gmm_tn(a, b):
    return _gdot(a, b, ((1,), (1,)))


def _flat_groups(x):
    return jnp.stack([x[:, j * RWKV_GROUP_W:(j + 1) * RWKV_GROUP_W] for j in range(RWKV_GROUPS)], axis=0)


def _head_rows(x):
    lane_head = lax.broadcasted_iota(jnp.int32, (1, RWKV_GROUP_W), 1) // HEAD_DIM
    zero = jnp.zeros((), x.dtype)
    groups = []
    for j in range(RWKV_GROUPS):
        xg = x[:, j * RWKV_GROUP_W:(j + 1) * RWKV_GROUP_W]
        groups.append(jnp.concatenate([jnp.where(lane_head == i, xg, zero) for i in range(RWKV_GROUP_HEADS)], axis=0))
    return jnp.stack(groups, axis=0)


def _rwkv_kernel(p_ref, mu_ref, w0_ref, a0_ref, kk_ref, ka_ref, rk_ref, gnw_ref, gnb_ref,
                 lora_ref, hb_ref, tril_ref, o_ref, prev_ref, s_ref, *, n_chunks):
    C, D, W = RWKV_CHUNK, RWKV_DIM, RWKV_GROUP_W
    tt = n_chunks * C

    @pl.when(pl.program_id(1) == 0)
    def _():
        prev_ref[...] = jnp.zeros_like(prev_ref)
        s_ref[...] = jnp.zeros_like(s_ref)

    p = p_ref[...].reshape(RWKV_PAIR * tt, RWKV_COLS)
    row = lax.broadcasted_iota(jnp.int32, p.shape, 0)
    prev = pltpu.roll(p, 1, 0)
    for i in range(RWKV_PAIR):
        prev = jnp.where(row == i * tt, prev_ref[i], prev)
        prev_ref[i] = p[(i + 1) * tt - 1:(i + 1) * tt, :]
    xs = p + (prev - p) * mu_ref[...]
    lo = xs[:, 3 * D:3 * D + LORA_COLS]
    lane = lax.broadcasted_iota(jnp.int32, lo.shape, 1)
    act = jnp.where(lane < DECAY_LORA, jnp.tanh(lo),
                    jnp.where(lane < DECAY_LORA + ICLR_LORA, lo, jax.nn.sigmoid(lo)))
    proj = jnp.dot(act.astype(BF16), lora_ref[...], preferred_element_type=F32)

    def pair(x):
        return jnp.concatenate([x[i * tt:(i + 1) * tt] for i in range(RWKV_PAIR)], axis=1)

    def head_sum(x):
        xb = x.astype(BF16)
        return jnp.concatenate([jnp.dot(xb[:, j * W:(j + 1) * W], hb_ref[...], preferred_element_type=F32)
                                for j in range(RWKV_GROUPS)], axis=1)

    r, k, v = pair(xs[:, 0:D]), pair(xs[:, D:2 * D]), pair(xs[:, 2 * D:3 * D])
    z = -(w0_ref[...] + pair(proj[:, 0:D]))
    softplus = jnp.maximum(z, 0.0) + jnp.log(1.0 + jnp.exp(-jnp.abs(z)))
    logw = -jnp.exp(-softplus - 0.5)
    a = jax.nn.sigmoid(a0_ref[...] + pair(proj[:, D:2 * D]))
    g = pair(proj[:, 2 * D:3 * D])
    kk = k * kk_ref[...]
    kk = kk / jnp.maximum(jnp.sqrt(head_sum(kk * kk)), 1e-12)
    k = k * (1.0 + (a - 1.0) * ka_ref[...])

    logw_hi = logw.astype(BF16)
    logw_lo = (logw - logw_hi.astype(F32)).astype(BF16)
    cum = (jnp.dot(tril_ref[...], logw_hi, preferred_element_type=F32)
           + jnp.dot(tril_ref[...], logw_lo, preferred_element_type=F32))
    e_pos = jnp.exp(cum)
    e_neg = jnp.exp(-cum)
    kb = kk * a
    r_t = (r * e_pos).astype(BF16)
    k_t = (k * e_neg).astype(BF16)
    b_t = (kb * e_neg).astype(BF16)
    a_t = (-kk * jnp.exp(cum - logw)).astype(BF16)
    v_b = v.astype(BF16)

    r64 = lax.broadcasted_iota(jnp.int32, (RWKV_GROUP_HEADS * C, W), 0) % C
    c64 = lax.broadcasted_iota(jnp.int32, (RWKV_GROUP_HEADS * C, W), 1) % C
    strict = (r64 > c64)[None]
    eye = (lax.broadcasted_iota(jnp.int32, (RWKV_GROUP_HEADS * C, W), 0)
           == lax.broadcasted_iota(jnp.int32, (RWKV_GROUP_HEADS * C, W), 1))[None].astype(F32)
    incl = (lax.broadcasted_iota(jnp.int32, (C, W), 0) >= lax.broadcasted_iota(jnp.int32, (C, W), 1) % C)[None]

    def pair_lanes(x, c):
        return x[c * C:(c + 1) * C]

    ys = []
    for c in range(n_chunks):
        g_end = pair_lanes(e_pos, c)[C - 1:C, :]
        k_end = (pair_lanes(k * e_neg, c) * g_end).astype(BF16)
        b_end = (pair_lanes(kb * e_neg, c) * g_end).astype(BF16)
        ax, bx, kx = _head_rows(pair_lanes(a_t, c)), _head_rows(pair_lanes(b_t, c)), _head_rows(pair_lanes(k_t, c))
        vx = _head_rows(pair_lanes(v_b, c))
        rf = _flat_groups(pair_lanes(r_t, c))
        sc = _gmm_nt(jnp.concatenate([ax, rf], axis=1), jnp.concatenate([bx, kx], axis=1))
        a_ab = jnp.where(strict, sc[:, :4 * C, :W], 0.0)
        a_ak = jnp.where(strict, sc[:, :4 * C, W:], 0.0)
        a_rb = jnp.where(incl, sc[:, 4 * C:, :W], 0.0)
        a_rk = jnp.where(incl, sc[:, 4 * C:, W:], 0.0)
        inv = eye + a_ab
        powr = _gmm(a_ab, a_ab)
        span = 2
        while span < C:
            if 2 * span < C:
                both = _gmm(powr, jnp.concatenate([inv.astype(BF16), powr.astype(BF16)], axis=-1))
                inv, powr = inv + both[..., :W], both[..., W:]
            else:
                inv = inv + _gmm(powr, inv)
            span *= 2
        akv = _gmm(a_ak, vx)

        s = s_ref[...]
        sr = _gmm_nt(jnp.concatenate([ax, rf], axis=1), s)
        u = _gmm(inv, sr[:, :4 * C] + akv)
        uv = jnp.concatenate([u.astype(BF16), vx], axis=1)
        y = sr[:, 4 * C:] + _gmm(jnp.concatenate([a_rb, a_rk], axis=-1), uv)
        s_ref[...] = (s * _flat_groups(g_end)
                      + _gmm_tn(uv, jnp.concatenate([_head_rows(b_end), _head_rows(k_end)], axis=1)))
        ys.append(jnp.concatenate([y[j] for j in range(RWKV_GROUPS)], axis=-1))

    y = jnp.concatenate(ys, axis=0)
    inv_n = 1.0 / HEAD_DIM
    yc = y - head_sum(y) * inv_n
    var = head_sum(yc * yc) * inv_n
    y = yc * lax.rsqrt(var + GN_EPS) * gnw_ref[...] + gnb_ref[...]
    out = (y + head_sum(r * k * rk_ref[...]) * v) * g
    for i in range(RWKV_PAIR):
        o_ref[i] = out[:, i * D:(i + 1) * D].astype(o_ref.dtype)


def _mix_kernel(sink_ref, bias_ref, q_ref, kp_ref, kc_ref, vp_ref, vc_ref, *rest, n_chunks):
    rwkv_in, (ya_ref, yr_ref, prev_ref, s_ref) = rest[:-4], rest[-4:]
    first = pl.program_id(1) == 0
    pair = range(RWKV_PAIR)
    outs = _attn_body(first, [q_ref[i] for i in pair], [kp_ref[i] for i in pair], [kc_ref[i] for i in pair],
                      [vp_ref[i] for i in pair], [vc_ref[i] for i in pair], bias_ref[...], sink_ref[...])
    for i in pair:
        ya_ref[i] = outs[i].astype(ya_ref.dtype)
    _rwkv_kernel(*rwkv_in, yr_ref, prev_ref, s_ref, n_chunks=n_chunks)


def _mix_call(q, k, v, bias, sinks, rw, mu, w0, w2, a0, a2, g2, k_k, k_a, r_k, gn_w, gn_b):
    bsz, t, cols = rw.shape
    C, D = RWKV_CHUNK, RWKV_DIM
    tt = ATT_BLOCKS * WINDOW
    n_chunks = tt // C
    lora = jnp.zeros((LORA_COLS, 3 * D), F32)
    lora = lora.at[0:DECAY_LORA, 0:D].set(w2)
    lora = lora.at[DECAY_LORA:DECAY_LORA + ICLR_LORA, D:2 * D].set(a2)
    lora = lora.at[DECAY_LORA + ICLR_LORA:, 2 * D:].set(g2)
    head_id = jnp.arange(RWKV_GROUP_W) // HEAD_DIM
    hb = (head_id[:, None] == head_id[None, :]).astype(BF16)
    pos = jnp.arange(tt)
    tril = ((pos[:, None] >= pos[None, :]) & (pos[:, None] // C == pos[None, :] // C)).astype(BF16)
    vec = lambda x: x.reshape(1, -1).astype(F32)
    both = lambda x: jnp.tile(vec(x), (1, RWKV_PAIR))
    vecs = [vec(mu), both(w0), both(a0), both(k_k), both(k_a), both(r_k), both(gn_w), both(gn_b)]
    consts = vecs + [lora.astype(BF16), hb, tril]
    cur = lambda b, c: (b, c, 0)
    prv = lambda b, c: (b, jnp.maximum(c * ATT_BLOCKS - 1, 0), 0)
    cur_kv = pl.BlockSpec((RWKV_PAIR, tt, ATT_KV_DIM), cur)
    prv_kv = pl.BlockSpec((RWKV_PAIR, WINDOW, ATT_KV_DIM), prv)
    head_blk = pl.BlockSpec((RWKV_PAIR, tt, D), cur)
    sinks = sinks.reshape(ATT_HEADS, 1, 1).astype(F32)
    return pl.pallas_call(
        functools.partial(_mix_kernel, n_chunks=n_chunks),
        grid=(bsz // RWKV_PAIR, t // tt),
        in_specs=[_const_spec(sinks.shape), _const_spec(bias.shape),
                  pl.BlockSpec((RWKV_PAIR, tt, ATT_DIM), cur), prv_kv, cur_kv, prv_kv, cur_kv,
                  pl.BlockSpec((RWKV_PAIR, tt, cols), cur)] + [_const_spec(x.shape) for x in consts],
        out_specs=[head_blk, head_blk],
        out_shape=[jax.ShapeDtypeStruct((bsz, t, D), BF16)] * 2,
        scratch_shapes=[pltpu.VMEM((RWKV_PAIR, 1, cols), F32),
                        pltpu.VMEM((RWKV_GROUPS, RWKV_GROUP_W, RWKV_GROUP_W), F32)],
        compiler_params=_params("parallel", "arbitrary"),
    )(sinks, bias, q, k, k, v, v, rw, *consts)


SSM_TILE_GROUPS = 2
SSM_TILES = SSM_GROUPS // SSM_TILE_GROUPS
SSM_HALF = SSM_TILE_GROUPS * SSM_STATE
SSM_LANE_TILE = SSM_DIM // 2


def _s5_operators(lam_re, lam_im, log_dt, b_re, b_im, c_re, c_im, n_levels):
    L, G, P, H = SSM_CHUNK, SSM_GROUPS, SSM_STATE, SSM_GROUP_CH
    hi = lax.Precision.HIGHEST
    lr = jnp.minimum(lam_re.astype(F32), -1e-4)
    li = lam_im.astype(F32)
    dt = jnp.exp(log_dt.astype(F32))[:, None]
    n = jnp.arange(L + 1, dtype=F32)[:, None, None]
    mag = jnp.exp(lr * dt * n)
    pr = mag * jnp.cos(li * dt * n)
    pi = mag * jnp.sin(li * dt * n)
    den = lr * lr + li * li
    nr, ni = pr[1] - 1.0, pi[1]
    fr = (nr * lr + ni * li) / den
    fi = (ni * lr - nr * li) / den
    b_re, b_im = b_re.astype(F32), b_im.astype(F32)
    bb_re = fr[..., None] * b_re - fi[..., None] * b_im
    bb_im = fr[..., None] * b_im + fi[..., None] * b_re
    c_re, c_im = c_re.astype(F32), c_im.astype(F32)
    cl_re = c_re[None] * pr[:L, :, None, :] - c_im[None] * pi[:L, :, None, :]
    cl_im = c_re[None] * pi[:L, :, None, :] + c_im[None] * pr[:L, :, None, :]
    kern = (jnp.einsum('nghp,gpk->nghk', cl_re, bb_re, precision=hi)
            - jnp.einsum('nghp,gpk->nghk', cl_im, bb_im, precision=hi))
    eye = jnp.eye(G, dtype=F32)
    k_lag = jnp.einsum('lgab,gk->lgbka', kern, eye).reshape(L, G * H, G * H)

    flat = lambda x: x.reshape(x.shape[0], 1, G * P)
    b_fr = jnp.einsum('gph,gk->ghkp', bb_re, eye).reshape(1, G * H, G * P)
    b_fi = jnp.einsum('gph,gk->ghkp', bb_im, eye).reshape(1, G * H, G * P)
    rev_r, rev_i = flat(pr[:L][::-1]), flat(pi[:L][::-1])
    tile_cols = lambda m: m.reshape(L, G * H, SSM_TILES, SSM_HALF)
    b_end = jnp.concatenate([tile_cols(b_fr * rev_r - b_fi * rev_i), tile_cols(b_fi * rev_r + b_fr * rev_i)],
                            axis=-1).reshape(L * G * H, SSM_TILES * 2 * SSM_HALF)
    c_fr = jnp.einsum('ghp,gk->gpkh', c_re, eye).reshape(1, G * P, G * H)
    c_fi = jnp.einsum('ghp,gk->gpkh', c_im, eye).reshape(1, G * P, G * H)
    fwd_r = pr[1:].reshape(L, G * P, 1)
    fwd_i = pi[1:].reshape(L, G * P, 1)
    tile_rows = lambda m: m.reshape(L, SSM_TILES, SSM_HALF, G * H)
    c_end = jnp.concatenate([tile_rows(c_fr * fwd_r - c_fi * fwd_i), -tile_rows(c_fr * fwd_i + c_fi * fwd_r)],
                            axis=2)
    c_end = jnp.transpose(c_end, (1, 2, 0, 3)).reshape(SSM_TILES * 2 * SSM_HALF, L * G * H)
    lam = lambda x: x.reshape(SSM_TILES, 1, SSM_HALF)
    qr, qi = pr[L], pi[L]
    scan_r, scan_i = [], []
    for _ in range(n_levels):
        scan_r.append(lam(qr))
        scan_i.append(lam(qi))
        qr, qi = qr * qr - qi * qi, 2.0 * qr * qi
    lt, st = SSM_LANE_TILE, SSM_TILES * SSM_HALF
    b_end = b_end.reshape(L, G * H, 2 * st)
    b_end = jnp.stack([b_end[:, i * lt:(i + 1) * lt, i * st:(i + 1) * st].reshape(L * lt, st) for i in range(2)])
    c_end = c_end.reshape(2 * st, L, G * H)
    c_end = jnp.stack([c_end[i * st:(i + 1) * st, :, i * lt:(i + 1) * lt].reshape(st, L * lt) for i in range(2)])
    return (k_lag.astype(BF16), b_end.astype(BF16), c_end.astype(BF16), jnp.stack(scan_r), jnp.stack(scan_i))


def _cmul(x, cr, ci):
    re, im = x[:, :SSM_HALF], x[:, SSM_HALF:]
    return jnp.concatenate([cr * re - ci * im, cr * im + ci * re], axis=1)


def _s5_kernel(ulo_ref, uhi_ref, klag_ref, bend_ref, cend_ref, scanr_ref, scani_ref, d_ref,
               gluw_ref, glub_ref, olo_ref, ohi_ref, utb_ref, ucm_ref, acc_ref, xin_ref, *, n_chunks, n_levels):
    L, nc, W, lt = SSM_CHUNK, n_chunks, SSM_DIM, SSM_LANE_TILE
    tile_w = 2 * SSM_HALF
    for s in range(L):
        halves = [ref[pl.ds(s, nc, stride=L), :] for ref in (ulo_ref, uhi_ref)]
        rows = jnp.concatenate(halves, axis=1)
        acc_ref[s * nc:(s + 1) * nc, :] = d_ref[...] * rows
        utb_ref[s * nc:(s + 1) * nc, :] = rows.astype(BF16)
        for i in range(2):
            ucm_ref[i, :, s * lt:(s + 1) * lt] = halves[i].astype(BF16)
    for lag in range(L):
        acc_ref[lag * nc:, :] += jnp.dot(utb_ref[0:(L - lag) * nc, :], klag_ref[lag], preferred_element_type=F32)
    e = jnp.concatenate([jnp.dot(ucm_ref[i], bend_ref[i], preferred_element_type=F32) for i in range(2)], axis=1)
    row = lax.broadcasted_iota(jnp.int32, (nc, tile_w), 0)
    for j in range(SSM_TILES):
        x = e[:, j * tile_w:(j + 1) * tile_w]
        for k in range(n_levels):
            sh = jnp.where(row >= (1 << k), pltpu.roll(x, 1 << k, 0), 0.0)
            x = x + _cmul(sh, scanr_ref[k, j], scani_ref[k, j])
        xin = jnp.where(row >= 1, pltpu.roll(x, 1, 0), 0.0)
        xin_ref[:, j * tile_w:(j + 1) * tile_w] = xin.astype(BF16)
    half = SSM_TILES * SSM_HALF
    ys = [jnp.dot(xin_ref[:, i * half:(i + 1) * half], cend_ref[i], preferred_element_type=F32)
          for i in range(2)]
    for t in range(L):
        y_state = jnp.concatenate([y[:, t * lt:(t + 1) * lt] for y in ys], axis=1)
        y = jax.nn.gelu(acc_ref[t * nc:(t + 1) * nc, :] + y_state)
        gate = jnp.dot(y.astype(BF16), gluw_ref[...], preferred_element_type=F32) + glub_ref[...]
        y = y * jax.nn.sigmoid(gate)
        olo_ref[pl.ds(t, nc, stride=L), :] = y[:, :lt]
        ohi_ref[pl.ds(t, nc, stride=L), :] = y[:, lt:]


def _s5_call(su_lo, su_hi, ops, d, glu_w, glu_b):
    bsz, t, half = su_lo.shape
    width = 2 * half
    nc = t // SSM_CHUNK
    n_levels = ops[-1].shape[0]
    consts = list(ops) + [d.reshape(1, width).astype(F32), glu_w, glu_b.reshape(1, width).astype(F32)]
    kern = functools.partial(_s5_kernel, n_chunks=nc, n_levels=n_levels)
    seq = pl.BlockSpec((None, t, half), lambda b: (b, 0, 0))
    return pl.pallas_call(
        kern,
        grid=(bsz,),
        in_specs=[seq, seq] + [_const_spec(x.shape) for x in consts],
        out_specs=[seq, seq],
        out_shape=[jax.ShapeDtypeStruct((bsz, t, half), F32)] * 2,
        scratch_shapes=[pltpu.VMEM((t, width), BF16), pltpu.VMEM((2, nc, SSM_CHUNK * half), BF16),
                        pltpu.VMEM((t, width), F32), pltpu.VMEM((nc, SSM_TILES * 2 * SSM_HALF), BF16)],
        compiler_params=_params("parallel"),
    )(su_lo, su_hi, *consts)


def kernel(x, rel_bias, ln_pre_ffn1, ln_post_ffn1, ffn1_w_gate, ffn1_w_up, ffn1_w_down, ln_pre_mix, ln_post_mix, w_in, w_out, att_sinks, rwkv_mu, rwkv_w0, rwkv_w2, rwkv_a0, rwkv_a2, rwkv_g2, rwkv_k_k, rwkv_k_a, rwkv_r_k, rwkv_gn_w, rwkv_gn_b, ssm_lambda_re, ssm_lambda_im, ssm_log_dt, ssm_b_re, ssm_b_im, ssm_c_re, ssm_c_im, ssm_d, ssm_glu_w, ssm_glu_b, ln_pre_ffn2, ln_post_ffn2, ffn2_w_gate, ffn2_w_up, ffn2_w_down):
    bsz, t, d = x.shape
    n = bsz * t
    depth = w_in.shape[0]
    tm = 512 if n % 512 == 0 else 128
    ff_chunk = 256
    bf = lambda w: w.astype(BF16)
    ffn1 = (ln_pre_ffn1, ln_post_ffn1, bf(ffn1_w_gate), bf(ffn1_w_up), bf(ffn1_w_down))
    ffn2 = (ln_pre_ffn2, ln_post_ffn2, bf(ffn2_w_gate), bf(ffn2_w_up), bf(ffn2_w_down))
    w_in_b, w_out_b, glu_w_b = bf(w_in), bf(w_out), bf(ssm_glu_w)
    bias = _band_bias(rel_bias)
    h = x.reshape(n, d)
    seq = lambda z: z.reshape(bsz, t, z.shape[-1])
    flat = lambda z: z.reshape(n, z.shape[-1])
    n_levels = max(1, math.ceil(math.log2(t // SSM_CHUNK)))
    for l in range(depth):
        h, q, k, v, rw, su_lo, su_hi = _ffn_call(h, l, *ffn1, tm=tm, ff_chunk=ff_chunk, proj=(ln_pre_mix, w_in_b))
        ya, yr = _mix_call(seq(q), seq(k), seq(v), bias, att_sinks[l], seq(rw), rwkv_mu[l], rwkv_w0[l], rwkv_w2[l],
                           rwkv_a0[l], rwkv_a2[l], rwkv_g2[l], rwkv_k_k[l], rwkv_k_a[l], rwkv_r_k[l],
                           rwkv_gn_w[l], rwkv_gn_b[l])
        ops = _s5_operators(ssm_lambda_re[l], ssm_lambda_im[l], ssm_log_dt[l], ssm_b_re[l], ssm_b_im[l],
                            ssm_c_re[l], ssm_c_im[l], n_levels=n_levels)
        ys_lo, ys_hi = _s5_call(seq(su_lo), seq(su_hi), ops, ssm_d[l], glu_w_b[l], ssm_glu_b[l])
        h = _ffn_call(h, l, *ffn2, tm=tm, ff_chunk=ff_chunk,
                      mix=([flat(ya), flat(yr), flat(ys_lo), flat(ys_hi)], w_out_b, ln_post_mix))
    return h.reshape(bsz, t, d)
```
